```python
import math
import jax, jax.numpy as jnp
from jax import lax
import numpy as np

D_MODEL = 2048
BATCH = 2
SEQ = 4096
DEPTH = 1
DEC_BATCH = 32
DEC_SEQ = 1
PAST_LEN = 16384
PAGE_SIZE = 128

MIX_WIDTH = D_MODEL
D_CONV = MIX_WIDTH // 2
CONV_W = 3
D_ATT = MIX_WIDTH - D_CONV
N_ATT_HEADS = 8
DV = D_ATT // N_ATT_HEADS
DK = DV // 2
QBLOCK = 128
N_KEYS = 128
N_EXPERTS = N_KEYS * N_KEYS
PK_HEADS = 8
D_QUERY = 256
DQ_HALF = D_QUERY // 2
PK_TOPK = 16
PEER_BLOCK = 128
EPS = 1e-6
SUBLN_EPS = 1e-5
NEG_INF = -1e30

kernel_name = 'hymba_conv_diffattn_peer_adaln_step'


def rms_norm(x, g, eps=EPS):
    x32 = x.astype(jnp.float32)
    y = x32 * lax.rsqrt(jnp.mean(x32 * x32, axis=-1, keepdims=True) + eps)
    return (y * g.astype(jnp.float32)).astype(x.dtype)


def alibi_slopes():
    return jnp.asarray(2.0 ** (-(8.0 / N_ATT_HEADS) * np.arange(1, N_ATT_HEADS + 1)), dtype=jnp.float32)


def diff_softmax_combine(s, dist, lam, slopes):
    bias = -slopes[:, None, None, None] * dist.astype(jnp.float32)
    s = jnp.where(dist >= 0, s + bias, NEG_INF)
    p = jax.nn.softmax(s, axis=-1)
    return p[..., 0, :, :] - lam * p[..., 1, :, :]


def diff_attn_prompt(q, k, v, lam, slopes):
    b, s_len = q.shape[0], q.shape[1]
    nb = s_len // QBLOCK
    qb = q.reshape(b, nb, QBLOCK, N_ATT_HEADS, 2, DK).swapaxes(0, 1)
    pos_k = jnp.arange(s_len)
    scale = DK ** -0.5

    def one_block(args):
        q_blk, i = args
        pos_q = i * QBLOCK + jnp.arange(QBLOCK)
        s = jnp.einsum('bqhmd,bkhmd->bhmqk', q_blk, k).astype(jnp.float32) * scale
        a = diff_softmax_combine(s, pos_q[:, None] - pos_k[None, :], lam, slopes)
        return jnp.einsum('bhqk,bkhd->bqhd', a.astype(v.dtype), v)

    o = lax.map(one_block, (qb, jnp.arange(nb)))
    return o.swapaxes(0, 1).reshape(b, s_len, N_ATT_HEADS, DV)


def diff_attn_sample(q, k_new, v_new, cache_k, cache_v, page_table, lam, slopes):
    scale = DK ** -0.5

    def one_seq(args):
        q_s, k_s, v_s, pages = args
        k_past = cache_k[pages].reshape(-1, N_ATT_HEADS, 2, DK)
        v_past = cache_v[pages].reshape(-1, N_ATT_HEADS, DV)
        past = k_past.shape[0]
        k_all = jnp.concatenate([k_past, k_s.astype(k_past.dtype)], axis=0)
        v_all = jnp.concatenate([v_past, v_s.astype(v_past.dtype)], axis=0)
        t_new = q_s.shape[0]
        pos_q = past + jnp.arange(t_new)
        pos_k = jnp.arange(past + t_new)
        s = jnp.einsum('qhmd,khmd->hmqk', q_s.astype(k_all.dtype), k_all).astype(jnp.float32) * scale
        a = diff_softmax_combine(s, pos_q[:, None] - pos_k[None, :], lam, slopes)
        return jnp.einsum('hqk,khd->qhd', a.astype(v_all.dtype), v_all)

    return lax.map(one_seq, (q, k_new, v_new, page_table))


def causal_conv(z_pad, w):
    t = z_pad.shape[1] - (CONV_W - 1)
    out = w[0] * z_pad[:, 0:t]
    for j in range(1, CONV_W):
        out = out + w[j] * z_pad[:, j:j + t]
    return out


def peer(h, w_query, sub_keys, expert_u, expert_v):
    lead = h.shape[:-1]
    x = h.reshape(-1, D_MODEL)
    n = x.shape[0]
    n_pad = (-n) % PEER_BLOCK
    x = jnp.pad(x, ((0, n_pad), (0, 0)))
    xb = x.reshape(-1, PEER_BLOCK, D_MODEL)

    def one_block(xt):
        q = (xt @ w_query).reshape(PEER_BLOCK, PK_HEADS, 2, DQ_HALF)
        s = jnp.einsum('thjd,hjnd->thjn', q, sub_keys).astype(jnp.float32)
        s_top, i_top = lax.top_k(s, PK_TOPK)
        cand = (s_top[:, :, 0, :, None] + s_top[:, :, 1, None, :]).reshape(PEER_BLOCK, PK_HEADS, PK_TOPK * PK_TOPK)
        c_top, c_idx = lax.top_k(cand, PK_TOPK)
        e1 = jnp.take_along_axis(i_top[:, :, 0], c_idx // PK_TOPK, axis=-1)
        e2 = jnp.take_along_axis(i_top[:, :, 1], c_idx % PK_TOPK, axis=-1)
        eid = e1 * N_KEYS + e2
        gate = jax.nn.softmax(c_top, axis=-1)
        act = jnp.einsum('td,thkd->thk', xt, expert_u[eid]).astype(jnp.float32)
        w = (gate * jax.nn.gelu(act, approximate=False)).astype(xt.dtype)
        return jnp.einsum('thk,thkd->td', w, expert_v[eid])

    y = lax.map(one_block, xb).reshape(-1, D_MODEL)[:n]
    return y.reshape(*lead, D_MODEL)


def hybrid_layer(x, c, conv_prev, attend, layer_idx, p):
    b, t = x.shape[0], x.shape[1]
    mod = jax.nn.silu(c) @ p['w_ada'] + p['b_ada']
    sh1, sc1, g1, sh2, sc2, g2 = jnp.split(mod[:, None, :], 6, axis=-1)
    h = rms_norm(x, p['norm1_g']) * (1 + sc1) + sh1
    proj = h @ p['w_in']
    cuts = [D_CONV, 2 * D_CONV, 3 * D_CONV, 3 * D_CONV + D_ATT, 3 * D_CONV + 2 * D_ATT]
    hc, gb, gc, q, k, v = jnp.split(proj, cuts, axis=-1)
    z = gc * hc
    z_pad = jnp.concatenate([conv_prev.astype(z.dtype), z], axis=1)
    conv_out = gb * causal_conv(z_pad, p['conv_w'])
    new_conv = z_pad[:, -(CONV_W - 1):]
    q = q.reshape(b, t, N_ATT_HEADS, 2, DK)
    k = k.reshape(b, t, N_ATT_HEADS, 2, DK)
    v = v.reshape(b, t, N_ATT_HEADS, DV)
    lam_init = 0.8 - 0.6 * math.exp(-0.3 * layer_idx)
    f32 = jnp.float32
    lam = (jnp.exp(jnp.sum(p['lambda_q1'].astype(f32) * p['lambda_k1'].astype(f32)))
           - jnp.exp(jnp.sum(p['lambda_q2'].astype(f32) * p['lambda_k2'].astype(f32))) + lam_init)
    att = attend(q, k, v, lam)
    att = rms_norm(att, p['subln_g'], SUBLN_EPS) * (1 - lam_init)
    mixed = jnp.concatenate([conv_out, att.reshape(b, t, D_ATT).astype(conv_out.dtype)], axis=-1) @ p['w_out']
    x = x + g1 * mixed
    h2 = rms_norm(x, p['norm2_g']) * (1 + sc2) + sh2
    x = x + g2 * peer(h2, p['w_query'], p['sub_keys'], p['expert_u'], p['expert_v'])
    return x, k.reshape(b, t, N_ATT_HEADS, 2 * DK), v, new_conv


def setup_inputs(seed: int = 0) -> dict:
    key = jax.random.key(seed)
    ks = jax.random.split(key, 32)
    n_pages = PAST_LEN // PAGE_SIZE
    used = DEC_BATCH * n_pages
    n_pool = used + max(1, used // 4)
    nrm = jax.random.normal
    f32 = jnp.float32
    page_table = jax.random.permutation(ks[0], n_pool)[:used].reshape(DEC_BATCH, n_pages).astype(jnp.int32)
    return {
        'x_prompt': nrm(ks[1], (BATCH, SEQ, D_MODEL), f32),
        'x_sample': nrm(ks[2], (DEC_BATCH, DEC_SEQ, D_MODEL), f32),
        'cache_k': nrm(ks[3], (n_pool, PAGE_SIZE, N_ATT_HEADS, 2 * DK), f32),
        'cache_v': nrm(ks[4], (n_pool, PAGE_SIZE, N_ATT_HEADS, DV), f32),
        'state_conv': nrm(ks[5], (DEC_BATCH, CONV_W - 1, D_CONV), f32),
        'page_table': page_table,
        'c_prompt': nrm(ks[6], (BATCH, D_MODEL), f32),
        'c_sample': nrm(ks[7], (DEC_BATCH, D_MODEL), f32),
        'norm1_g': 1.0 + 0.02 * nrm(ks[8], (D_MODEL,), f32),
        'norm2_g': 1.0 + 0.02 * nrm(ks[9], (D_MODEL,), f32),
        'w_ada': nrm(ks[10], (D_MODEL, 6 * D_MODEL), f32) * D_MODEL ** -0.5,
        'b_ada': 0.02 * nrm(ks[11], (6 * D_MODEL,), f32),
        'w_in': nrm(ks[12], (D_MODEL, 3 * D_CONV + 3 * D_ATT), f32) * D_MODEL ** -0.5,
        'conv_w': nrm(ks[13], (CONV_W, D_CONV), f32) * CONV_W ** -0.5,
        'lambda_q1': 0.1 * nrm(ks[14], (DK,), f32),
        'lambda_k1': 0.1 * nrm(ks[15], (DK,), f32),
        'lambda_q2': 0.1 * nrm(ks[16], (DK,), f32),
        'lambda_k2': 0.1 * nrm(ks[17], (DK,), f32),
        'subln_g': 1.0 + 0.02 * nrm(ks[18], (DV,), f32),
        'w_out': nrm(ks[19], (MIX_WIDTH, D_MODEL), f32) * MIX_WIDTH ** -0.5,
        'w_query': nrm(ks[20], (D_MODEL, PK_HEADS * D_QUERY), f32) * D_MODEL ** -0.5,
        'sub_keys': nrm(ks[21], (PK_HEADS, 2, N_KEYS, DQ_HALF), f32) * DQ_HALF ** -0.5,
        'expert_u': nrm(ks[22], (N_EXPERTS, D_MODEL), f32) * D_MODEL ** -0.5,
        'expert_v': nrm(ks[23], (N_EXPERTS, D_MODEL), f32),
        'final_g': 1.0 + 0.02 * nrm(ks[24], (D_MODEL,), f32),
    }


def reference(x_prompt, x_sample, cache_k, cache_v, state_conv, page_table, c_prompt, c_sample,
              norm1_g, norm2_g, w_ada, b_ada, w_in, conv_w, lambda_q1, lambda_k1, lambda_q2, lambda_k2,
              subln_g, w_out, w_query, sub_keys, expert_u, expert_v, final_g):
    p = dict(norm1_g=norm1_g, norm2_g=norm2_g, w_ada=w_ada, b_ada=b_ada, w_in=w_in, conv_w=conv_w,
             lambda_q1=lambda_q1, lambda_k1=lambda_k1, lambda_q2=lambda_q2, lambda_k2=lambda_k2,
             subln_g=subln_g, w_out=w_out, w_query=w_query, sub_keys=sub_keys,
             expert_u=expert_u, expert_v=expert_v)
    slopes = alibi_slopes()

    def prompt_attend(q, k, v, lam):
        return diff_attn_prompt(q, k, v, lam, slopes)

    def sample_attend(q, k, v, lam):
        return diff_attn_sample(q, k, v, cache_k, cache_v, page_table, lam, slopes)

    conv_zero = jnp.zeros((x_prompt.shape[0], CONV_W - 1, D_CONV), x_prompt.dtype)
    xp, xs = x_prompt, x_sample
    for layer_idx in range(DEPTH):
        xp, k_prompt, v_prompt, conv_prompt = hybrid_layer(xp, c_prompt, conv_zero, prompt_attend, layer_idx, p)
        xs, k_sample, v_sample, conv_sample = hybrid_layer(xs, c_sample, state_conv, sample_attend, layer_idx, p)
    y_prompt = rms_norm(xp, final_g)
    y_sample = rms_norm(xs, final_g)
    return (y_prompt, y_sample, k_prompt, v_prompt, conv_prompt, k_sample, v_sample, conv_sample)
```

```python
import functools
import math

import jax
import jax.numpy as jnp
from jax import lax
from jax.experimental import pallas as pl
from jax.experimental.pallas import tpu as pltpu

F32 = jnp.float32
BF16 = jnp.bfloat16

N_ATT_HEADS = 8
DK = 64
DV = 128
CONV_W = 3
N_KEYS = 128
PK_HEADS = 8
PK_TOPK = 16
EPS = 1e-6
SUBLN_EPS = 1e-5
NEG_INF = -1e30
LAMBDA_INIT = 0.8 - 0.6 * math.exp(-0.3 * 0)
SQRT_HALF = 0.7071067811865476

LANES = 128
VMEM_LIMIT = 56 * 1024 * 1024


def _params(sem, vmem=VMEM_LIMIT):
    return pltpu.CompilerParams(dimension_semantics=sem, vmem_limit_bytes=vmem)


def _dot(a, b):
    return jnp.dot(a, b, preferred_element_type=F32)


def _dot_nt(a, b):
    return lax.dot_general(a, b, (((1,), (1,)), ((), ())), preferred_element_type=F32)


def _dot_tn(a, b):
    return lax.dot_general(a, b, (((0,), (0,)), ((), ())), preferred_element_type=F32)


def _rms(x, g, eps):
    return x * lax.rsqrt(jnp.mean(x * x, axis=-1, keepdims=True) + eps) * g


def _ada_kernel(c_ref, w_ref, b_ref, o_ref):
    c = c_ref[...]
    s = c * jax.nn.sigmoid(c)
    o_ref[...] = _dot(s.astype(BF16), w_ref[...].astype(BF16)) + b_ref[...]


def _ada(c_all, w_ada, b_ada, tn=1024):
    r, d = c_all.shape
    n = w_ada.shape[1]
    return pl.pallas_call(
        _ada_kernel,
        grid=(n // tn,),
        in_specs=[pl.BlockSpec((r, d), lambda j: (0, 0)),
                  pl.BlockSpec((d, tn), lambda j: (0, j)),
                  pl.BlockSpec((1, tn), lambda j: (0, j))],
        out_specs=pl.BlockSpec((r, tn), lambda j: (0, j)),
        out_shape=jax.ShapeDtypeStruct((r, n), F32),
        compiler_params=_params(("arbitrary",)),
    )(c_all, w_ada, b_ada.reshape(1, n))


def _inproj_kernel(x_ref, g_ref, sc_ref, sh_ref, w_ref, proj_ref, qkv_ref, h_scr, *, q_tile):
    j = pl.program_id(1)

    @pl.when(j == 0)
    def _():
        y = _rms(x_ref[...], g_ref[...], EPS)
        h_scr[...] = (y * (1.0 + sc_ref[0]) + sh_ref[0]).astype(BF16)

    p = _dot(h_scr[...], w_ref[...])
    proj_ref[...] = p

    @pl.when(j == q_tile)
    def _():
        qkv_ref[...] = (p * (DK ** -0.5)).astype(BF16)

    @pl.when(j > q_tile)
    def _():
        qkv_ref[...] = p.astype(BF16)


def _inproj(x, g, mod3, mod_row, r, w_bf, tm):
    m, d = x.shape
    n = w_bf.shape[1]
    tn = n // 6
    mod_spec = lambda c: pl.BlockSpec((1, r, d), lambda i, j: (mod_row(i), 0, c))
    return pl.pallas_call(
        functools.partial(_inproj_kernel, q_tile=3),
        grid=(m // tm, 6),
        in_specs=[pl.BlockSpec((tm, d), lambda i, j: (i, 0)),
                  pl.BlockSpec((1, d), lambda i, j: (0, 0)),
                  mod_spec(1), mod_spec(0),
                  pl.BlockSpec((d, tn), lambda i, j: (0, j))],
        out_specs=[pl.BlockSpec((tm, tn), lambda i, j: (i, j)),
                   pl.BlockSpec((tm, tn), lambda i, j: (i, jnp.maximum(j - 3, 0)))],
        out_shape=[jax.ShapeDtypeStruct((m, n), F32),
                   jax.ShapeDtypeStruct((m, n // 2), BF16)],
        scratch_shapes=[pltpu.VMEM((tm, d), BF16)],
        compiler_params=_params(("arbitrary", "arbitrary")),
    )(x, g.reshape(1, d), mod3, mod3, w_bf)


def _conv_prompt_kernel(hc_ref, gb_ref, gc_ref, w_ref, st_ref, o_ref, new_ref, z_scr, *, tm, tiles_per_seq):
    i = pl.program_id(0)
    pad = 8

    @pl.when(i % tiles_per_seq == 0)
    def _():
        z_scr[pad - 2:pad, :] = st_ref[0]

    z = gc_ref[...] * hc_ref[...]
    z_scr[pad:pad + tm, :] = z
    w = w_ref[...]
    conv = (w[0:1] * z_scr[pad - 2:pad - 2 + tm, :]
            + w[1:2] * z_scr[pad - 1:pad - 1 + tm, :]
            + w[2:3] * z)
    o_ref[...] = (gb_ref[...] * conv).astype(BF16)
    last = z_scr[pad + tm - 2:pad + tm, :]
    new_ref[0] = last
    z_scr[pad - 2:pad, :] = last


def _conv_prompt(proj, conv_w, state, seq, tm):
    m = proj.shape[0]
    dc = conv_w.shape[1]
    nb = state.shape[0]
    tps = seq // tm
    col = lambda c: pl.BlockSpec((tm, dc), lambda i: (i, c))
    return pl.pallas_call(
        functools.partial(_conv_prompt_kernel, tm=tm, tiles_per_seq=tps),
        grid=(m // tm,),
        in_specs=[col(0), col(1), col(2),
                  pl.BlockSpec((CONV_W, dc), lambda i: (0, 0)),
                  pl.BlockSpec((1, CONV_W - 1, dc), lambda i: (i // tps, 0, 0))],
        out_specs=[pl.BlockSpec((tm, dc), lambda i: (i, 0)),
                   pl.BlockSpec((1, CONV_W - 1, dc), lambda i: (i // tps, 0, 0))],
        out_shape=[jax.ShapeDtypeStruct((m, dc), BF16),
                   jax.ShapeDtypeStruct((nb, CONV_W - 1, dc), F32)],
        scratch_shapes=[pltpu.VMEM((tm + 8, dc), F32)],
        compiler_params=_params(("arbitrary",)),
    )(proj, proj, proj, conv_w, state)


def _conv_sample_kernel(hc_ref, gb_ref, gc_ref, w_ref, s0_ref, s1_ref, o_ref, z_ref):
    z = gc_ref[...] * hc_ref[...]
    w = w_ref[...]
    conv = w[0:1] * s0_ref[...] + w[1:2] * s1_ref[...] + w[2:3] * z
    o_ref[...] = (gb_ref[...] * conv).astype(BF16)
    z_ref[...] = z


def _conv_sample(proj, conv_w, s0, s1):
    m = proj.shape[0]
    dc = conv_w.shape[1]
    col = lambda c: pl.BlockSpec((m, dc), lambda i: (0, c))
    full = pl.BlockSpec((m, dc), lambda i: (0, 0))
    return pl.pallas_call(
        _conv_sample_kernel,
        grid=(1,),
        in_specs=[col(0), col(1), col(2),
                  pl.BlockSpec((CONV_W, dc), lambda i: (0, 0)), full, full],
        out_specs=[full, full],
        out_shape=[jax.ShapeDtypeStruct((m, dc), BF16),
                   jax.ShapeDtypeStruct((m, dc), F32)],
        compiler_params=_params(("arbitrary",)),
    )(proj, proj, proj, conv_w, s0, s1)


def _lambda(lam_ref):
    lp = lam_ref[...]
    e1 = jnp.exp(jnp.sum(lp[0:1] * lp[1:2], axis=-1, keepdims=True))
    e2 = jnp.exp(jnp.sum(lp[2:3] * lp[3:4], axis=-1, keepdims=True))
    return e1 - e2 + LAMBDA_INIT


def _attn_prompt_kernel(slope_ref, lam_ref, q_ref, k_ref, v_ref, g_ref, o_ref,
                        m_scr, l_scr, acc_scr, *, tq):
    h = pl.program_id(1)
    i = pl.program_id(2)
    slope = slope_ref[h]
    q = q_ref[...]
    lane = lax.broadcasted_iota(jnp.int32, q.shape, 1)
    zero = jnp.zeros_like(q)
    qq = jnp.concatenate([jnp.where(lane < DK, q, zero),
                          jnp.where(lane >= DK, q, zero)], axis=0)
    krow = lax.broadcasted_iota(jnp.int32, (tq, 2 * tq), 0)
    qcol = lax.broadcasted_iota(jnp.int32, (tq, 2 * tq), 1)
    qcol = jnp.where(qcol >= tq, qcol - tq, qcol)
    bias = slope * krow.astype(F32)

    m_scr[...] = jnp.full(m_scr.shape, NEG_INF, F32)
    l_scr[...] = jnp.zeros(l_scr.shape, F32)
    acc_scr[...] = jnp.zeros(acc_scr.shape, F32)

    def block(j, masked):
        start = pl.multiple_of(j * tq, tq)
        kb = k_ref[pl.ds(start, tq), :]
        vb = v_ref[pl.ds(start, tq), :]
        s = _dot_nt(kb, qq) + bias
        if masked:
            s = jnp.where(krow > qcol, NEG_INF, s)
        off = slope * ((j - i) * tq).astype(F32)
        m_old = m_scr[...]
        m_new = jnp.maximum(m_old, jnp.max(s, axis=0, keepdims=True) + off)
        p = jnp.exp(s - (m_new - off))
        alpha = jnp.exp(m_old - m_new)
        l_scr[...] = alpha * l_scr[...] + jnp.sum(p, axis=0, keepdims=True)
        acc_scr[...] = alpha * acc_scr[...] + _dot_tn(vb, p.astype(BF16))
        m_scr[...] = m_new

    def body(j, carry):
        block(j, False)
        return carry

    lax.fori_loop(0, i, body, 0)
    block(i, True)

    o = acc_scr[...] / l_scr[...]
    o = o[:, :tq] - _lambda(lam_ref) * o[:, tq:]
    o = o * lax.rsqrt(jnp.mean(o * o, axis=0, keepdims=True) + SUBLN_EPS)
    o_ref[...] = ((o.T * g_ref[...]) * (1.0 - LAMBDA_INIT)).astype(BF16)


def _attn_prompt(qkv, lam_p, slopes, subln_g, nb, seq, tq):
    m = qkv.shape[0]
    nh = N_ATT_HEADS
    nq = seq // tq
    return pl.pallas_call(
        functools.partial(_attn_prompt_kernel, tq=tq),
        grid=(nb, nh, nq),
        in_specs=[pl.BlockSpec(memory_space=pltpu.SMEM),
                  pl.BlockSpec((4, DK), lambda b, h, i: (0, 0)),
                  pl.BlockSpec((tq, 2 * DK), lambda b, h, i: (b * nq + i, h)),
                  pl.BlockSpec((seq, 2 * DK), lambda b, h, i: (b, nh + h)),
                  pl.BlockSpec((seq, DV), lambda b, h, i: (b, 2 * nh + h)),
                  pl.BlockSpec((1, DV), lambda b, h, i: (0, 0))],
        out_specs=pl.BlockSpec((tq, DV), lambda b, h, i: (b * nq + i, h)),
        out_shape=jax.ShapeDtypeStruct((m, nh * DV), BF16),
        scratch_shapes=[pltpu.VMEM((1, 2 * tq), F32), pltpu.VMEM((1, 2 * tq), F32),
                        pltpu.VMEM((DV, 2 * tq), F32)],
        compiler_params=_params(("arbitrary", "arbitrary", "arbitrary")),
    )(slopes, lam_p, qkv, qkv, qkv, subln_g.reshape(1, DV))


def _dec_scores_kernel(pt_ref, slope_ref, lam_ref, q_ref, kn_ref, *rest, pages, page, past):
    k_refs = rest[:pages]
    a_ref, an_ref, s_scr = rest[pages:]
    j = pl.program_id(1)
    nh = N_ATT_HEADS
    width = nh * 2 * DK
    row = lax.broadcasted_iota(jnp.int32, (2 * nh, width), 0)
    col = lax.broadcasted_iota(jnp.int32, (2 * nh, width), 1)
    sel = (col // (2 * DK) == row % nh) & ((col // DK) % 2 == row // nh)
    q = q_ref[0].astype(F32)
    qmat = jnp.where(sel, q, 0.0).astype(BF16)
    for u in range(pages):
        kb = k_refs[u][0].astype(BF16)
        start = pl.multiple_of((j * pages + u) * page, page)
        s_scr[:, pl.ds(start, page)] = _dot_nt(qmat, kb)

    @pl.when(j == pl.num_programs(1) - 1)
    def _():
        kn = kn_ref[0].astype(BF16).astype(F32)
        s_new = jnp.sum(jnp.where(sel, q * kn, 0.0), axis=-1, keepdims=True)
        r1 = lax.broadcasted_iota(jnp.int32, (2 * nh, 1), 0) % nh
        slope = jnp.zeros((2 * nh, 1), F32)
        for hh in range(nh):
            slope = jnp.where(r1 == hh, slope_ref[hh], slope)
        kpos = lax.broadcasted_iota(jnp.int32, (2 * nh, past), 1)
        s = s_scr[...] - slope * (past - kpos).astype(F32)
        mx = jnp.maximum(jnp.max(s, axis=-1, keepdims=True), s_new)
        p = jnp.exp(s - mx)
        pn = jnp.exp(s_new - mx)
        inv = 1.0 / (jnp.sum(p, axis=-1, keepdims=True) + pn)
        lam = _lambda(lam_ref)
        p = p * inv
        pn = pn * inv
        a_ref[0] = p[:nh] - lam * p[nh:]
        an_ref[0] = jnp.broadcast_to(pn[:nh] - lam * pn[nh:], (nh, LANES))


def _dec_scores(page_table, slopes, lam_p, q_s, k_new, cache_k2, pages):
    db, n_pages = page_table.shape
    page, width = cache_k2.shape[1], cache_k2.shape[2]
    past = n_pages * page
    nh = N_ATT_HEADS
    kspec = lambda u: pl.BlockSpec((1, page, width),
                                   lambda b, j, pt: (pt[b, j * pages + u], 0, 0))
    grid_spec = pltpu.PrefetchScalarGridSpec(
        num_scalar_prefetch=1,
        grid=(db, n_pages // pages),
        in_specs=[pl.BlockSpec(memory_space=pltpu.SMEM),
                  pl.BlockSpec((4, DK), lambda b, j, pt: (0, 0)),
                  pl.BlockSpec((1, 1, width), lambda b, j, pt: (b, 0, 0)),
                  pl.BlockSpec((1, 1, width), lambda b, j, pt: (b, 0, 0))]
                 + [kspec(u) for u in range(pages)],
        out_specs=[pl.BlockSpec((1, nh, past), lambda b, j, pt: (b, 0, 0)),
                   pl.BlockSpec((1, nh, LANES), lambda b, j, pt: (b, 0, 0))],
        scratch_shapes=[pltpu.VMEM((2 * nh, past), F32)],
    )
    return pl.pallas_call(
        functools.partial(_dec_scores_kernel, pages=pages, page=page, past=past),
        grid_spec=grid_spec,
        out_shape=[jax.ShapeDtypeStruct((db, nh, past), F32),
                   jax.ShapeDtypeStruct((db, nh, LANES), F32)],
        compiler_params=_params(("arbitrary", "arbitrary")),
    )(page_table, slopes, lam_p, q_s.reshape(db, 1, width), k_new.reshape(db, 1, width),
      *([cache_k2] * pages))


def _dec_pv_kernel(pt_ref, a_ref, an_ref, vn_ref, g_ref, *rest, pages, page):
    v_refs = rest[:pages]
    o_ref, acc_scr = rest[pages:]
    j = pl.program_id(1)
    nh = N_ATT_HEADS
    width = nh * DV

    @pl.when(j == 0)
    def _():
        acc_scr[...] = jnp.zeros(acc_scr.shape, F32)

    hrow = lax.broadcasted_iota(jnp.int32, (2 * nh, width), 0)
    hcol = lax.broadcasted_iota(jnp.int32, (2 * nh, width), 1) // DV
    expand = (hrow == hcol).astype(F32).astype(BF16)
    pad = jnp.zeros((nh, page), F32)

    def spread(w):
        return _dot_tn(jnp.concatenate([w, pad], axis=0).astype(BF16), expand)

    acc = acc_scr[...]
    for u in range(pages):
        prod = spread(a_ref[0, :, u * page:(u + 1) * page]) * v_refs[u][0]
        acc = acc + jnp.sum(prod.reshape(page // 8, 8, width), axis=0)
    acc_scr[...] = acc

    @pl.when(j == pl.num_programs(1) - 1)
    def _():
        an_exp = spread(an_ref[0])[0:1]
        o = jnp.sum(acc, axis=0, keepdims=True) + an_exp * vn_ref[0]
        g = g_ref[...]
        for hh in range(nh):
            seg = o[:, hh * DV:(hh + 1) * DV]
            seg = seg * lax.rsqrt(jnp.mean(seg * seg, axis=-1, keepdims=True) + SUBLN_EPS)
            o_ref[0, :, hh * DV:(hh + 1) * DV] = (seg * g) * (1.0 - LAMBDA_INIT)


def _dec_pv(page_table, a, a_new, v_new, subln_g, cache_v2, pages):
    db, n_pages = page_table.shape
    page, width = cache_v2.shape[1], cache_v2.shape[2]
    nh = N_ATT_HEADS
    vspec = lambda u: pl.BlockSpec((1, page, width),
                                   lambda b, j, pt: (pt[b, j * pages + u], 0, 0))
    grid_spec = pltpu.PrefetchScalarGridSpec(
        num_scalar_prefetch=1,
        grid=(db, n_pages // pages),
        in_specs=[pl.BlockSpec((1, nh, pages * page), lambda b, j, pt: (b, 0, j)),
                  pl.BlockSpec((1, nh, LANES), lambda b, j, pt: (b, 0, 0)),
                  pl.BlockSpec((1, 1, width), lambda b, j, pt: (b, 0, 0)),
                  pl.BlockSpec((1, DV), lambda b, j, pt: (0, 0))]
                 + [vspec(u) for u in range(pages)],
        out_specs=pl.BlockSpec((1, 1, width), lambda b, j, pt: (b, 0, 0)),
        scratch_shapes=[pltpu.VMEM((8, width), F32)],
    )
    return pl.pallas_call(
        functools.partial(_dec_pv_kernel, pages=pages, page=page),
        grid_spec=grid_spec,
        out_shape=jax.ShapeDtypeStruct((db, 1, width), F32),
        compiler_params=_params(("arbitrary", "arbitrary")),
    )(page_table, a, a_new, v_new.reshape(db, 1, width), subln_g.reshape(1, DV),
      *([cache_v2] * pages)).reshape(db, width).astype(BF16)


def _outproj_kernel(x_ref, conv_ref, att_ref, w_ref, g1_ref, ng_ref, sc_ref, sh_ref,
                    x1_ref, h2_ref):
    dc = conv_ref.shape[1]
    mixed = _dot(conv_ref[...], w_ref[0:dc, :]) + _dot(att_ref[...], w_ref[dc:, :])
    x1 = x_ref[...] + g1_ref[0] * mixed
    x1_ref[...] = x1
    y = _rms(x1, ng_ref[...], EPS)
    h2_ref[...] = (y * (1.0 + sc_ref[0]) + sh_ref[0]).astype(BF16)


def _outproj(x, conv_o, att, w_bf, g2n, mod3, mod_row, r, tm):
    m, d = x.shape
    dc = conv_o.shape[1]
    mod_spec = lambda c: pl.BlockSpec((1, r, d), lambda i: (mod_row(i), 0, c))
    half = pl.BlockSpec((tm, dc), lambda i: (i, 0))
    row = pl.BlockSpec((tm, d), lambda i: (i, 0))
    return pl.pallas_call(
        _outproj_kernel,
        grid=(m // tm,),
        in_specs=[row, half, half,
                  pl.BlockSpec((d, d), lambda i: (0, 0)),
                  mod_spec(2),
                  pl.BlockSpec((1, d), lambda i: (0, 0)),
                  mod_spec(4), mod_spec(3)],
        out_specs=[row, row],
        out_shape=[jax.ShapeDtypeStruct((m, d), F32), jax.ShapeDtypeStruct((m, d), BF16)],
        compiler_params=_params(("arbitrary",)),
    )(x, conv_o, att, w_bf, mod3, g2n.reshape(1, d), mod3, mod3)


def _top_desc(s, k):
    out = []
    for _ in range(k):
        mx = jnp.max(s, axis=0, keepdims=True)
        out.append(mx)
        s = jnp.where(s == mx, NEG_INF, s)
    return out


_CAND_PAIRS = [(i, j) for i in range(PK_TOPK) for j in range(PK_TOPK)
               if (i + 1) * (j + 1) <= PK_TOPK]
_CAND_ROWS = -(-len(_CAND_PAIRS) // 8) * 8


def _peer_query_kernel(h2_ref, wq_ref, sk_ref, s1_ref, s2_ref, a1_ref, a2_ref, tau_ref,
                       cand_scr):
    q = _dot(h2_ref[...], wq_ref[...]).astype(BF16)
    dq = sk_ref.shape[-1]
    cand_scr[...] = jnp.full(cand_scr.shape, NEG_INF, F32)
    for h in range(PK_HEADS):
        s1 = _dot_nt(sk_ref[h, 0], q[:, (2 * h) * dq:(2 * h + 1) * dq])
        s2 = _dot_nt(sk_ref[h, 1], q[:, (2 * h + 1) * dq:(2 * h + 2) * dq])
        t1 = _top_desc(s1, PK_TOPK)
        t2 = _top_desc(s2, PK_TOPK)
        for r, (i, j) in enumerate(_CAND_PAIRS):
            cand_scr[r:r + 1, :] = t1[i] + t2[j]
        c = _top_desc(cand_scr[...], PK_TOPK)
        z = jnp.ones_like(c[0])
        for kk in range(1, PK_TOPK):
            z = z + jnp.exp(c[kk] - c[0])
        s1_ref[h] = s1
        s2_ref[h] = s2
        a1_ref[h] = jnp.exp(s1 - t1[0]) / z
        a2_ref[h] = jnp.exp(s2 - t2[0])
        tau_ref[h] = c[PK_TOPK - 1]


def _peer_query(h2, wq_bf, sk_bf, tt):
    n, d = h2.shape
    nq = wq_bf.shape[1]
    big = pl.BlockSpec((PK_HEADS, N_KEYS, tt), lambda i: (0, 0, i))
    shp = jax.ShapeDtypeStruct((PK_HEADS, N_KEYS, n), F32)
    return pl.pallas_call(
        _peer_query_kernel,
        grid=(n // tt,),
        in_specs=[pl.BlockSpec((tt, d), lambda i: (i, 0)),
                  pl.BlockSpec((d, nq), lambda i: (0, 0)),
                  pl.BlockSpec(sk_bf.shape, lambda i: (0, 0, 0, 0))],
        out_specs=[big, big, big, big,
                   pl.BlockSpec((PK_HEADS, 1, tt), lambda i: (0, 0, i))],
        out_shape=[shp, shp, shp, shp, jax.ShapeDtypeStruct((PK_HEADS, 1, n), F32)],
        scratch_shapes=[pltpu.VMEM((_CAND_ROWS, tt), F32)],
        compiler_params=_params(("arbitrary",)),
    )(h2, wq_bf, sk_bf)


def _peer_dense_kernel(h2_ref, s1_ref, s2_ref, a1_ref, a2_ref, tau_ref, u_ref, vt_ref,
                       y_ref, w_scr, *, rows):
    c = pl.program_id(1)

    @pl.when(c == 0)
    def _():
        y_ref[...] = jnp.zeros(y_ref.shape, F32)

    act = _dot_nt(u_ref[...], h2_ref[...])
    for e in range(rows):
        gate = jnp.zeros((N_KEYS, act.shape[1]), F32)
        for h in range(PK_HEADS):
            score = s1_ref[h, e:e + 1, :] + s2_ref[h]
            val = a1_ref[h, e:e + 1, :] * a2_ref[h]
            gate = gate + jnp.where(score >= tau_ref[h], val, 0.0)
        x = act[e * N_KEYS:(e + 1) * N_KEYS]
        gelu = 0.5 * x * (1.0 + lax.erf(x * SQRT_HALF))
        w_scr[e * N_KEYS:(e + 1) * N_KEYS, :] = (gate * gelu).astype(BF16)
    y_ref[...] += _dot(vt_ref[...], w_scr[...])


def _peer_dense(h2, s1, s2, a1, a2, tau, u_bf, vt_bf, tt, rows=8):
    n, d = h2.shape
    ne = u_bf.shape[0]
    eb = rows * N_KEYS
    big = pl.BlockSpec((PK_HEADS, N_KEYS, tt), lambda i, c: (0, 0, i))
    sub = pl.BlockSpec((PK_HEADS, rows, tt), lambda i, c: (0, c, i))
    return pl.pallas_call(
        functools.partial(_peer_dense_kernel, rows=rows),
        grid=(n // tt, ne // eb),
        in_specs=[pl.BlockSpec((tt, d), lambda i, c: (i, 0)),
                  sub, big, sub, big,
                  pl.BlockSpec((PK_HEADS, 1, tt), lambda i, c: (0, 0, i)),
                  pl.BlockSpec((eb, d), lambda i, c: (c, 0)),
                  pl.BlockSpec((d, eb), lambda i, c: (0, c))],
        out_specs=pl.BlockSpec((d, tt), lambda i, c: (0, i)),
        out_shape=jax.ShapeDtypeStruct((d, n), F32),
        scratch_shapes=[pltpu.VMEM((eb, tt), BF16)],
        compiler_params=_params(("arbitrary", "arbitrary")),
    )(h2, s1, s2, a1, a2, tau, u_bf, vt_bf)


def _final_kernel(x1_ref, yt_ref, g2_ref, fg_ref, o_ref):
    tm = x1_ref.shape[0]
    y = yt_ref[...].T[:tm]
    o_ref[...] = _rms(x1_ref[...] + g2_ref[0] * y, fg_ref[...], EPS)


def _final(x1, yt, final_g, mod3, mod_row, r, tm, tt):
    m, d = x1.shape
    return pl.pallas_call(
        _final_kernel,
        grid=(m // tm,),
        in_specs=[pl.BlockSpec((tm, d), lambda i: (i, 0)),
                  pl.BlockSpec((d, tt), lambda i: (0, i)),
                  pl.BlockSpec((1, r, d), lambda i: (mod_row(i), 0, 5)),
                  pl.BlockSpec((1, d), lambda i: (0, 0))],
        out_specs=pl.BlockSpec((tm, d), lambda i: (i, 0)),
        out_shape=jax.ShapeDtypeStruct((m, d), F32),
        compiler_params=_params(("arbitrary",)),
    )(x1, yt, mod3, final_g.reshape(1, d))


def _pick(n, pref):
    t = min(n, pref)
    assert n % t == 0, (n, t)
    return t


def kernel(x_prompt, x_sample, cache_k, cache_v, state_conv, page_table, c_prompt, c_sample,
           norm1_g, norm2_g, w_ada, b_ada, w_in, conv_w, lambda_q1, lambda_k1, lambda_q2,
           lambda_k2, subln_g, w_out, w_query, sub_keys, expert_u, expert_v, final_g):
    nb, seq, d = x_prompt.shape
    db = x_sample.shape[0]
    assert x_sample.shape[1] == 1
    n_pool, page = cache_k.shape[0], cache_k.shape[1]
    nh = N_ATT_HEADS
    dc = conv_w.shape[1]
    assert w_in.shape[1] == 6 * dc and dc == nh * DV and d == 2 * dc

    slopes = jnp.asarray(2.0 ** (-(8.0 / nh) * jnp.arange(1, nh + 1)), F32)
    lam_p = jnp.stack([lambda_q1, lambda_k1, lambda_q2, lambda_k2]).astype(F32)
    w_in_bf = w_in.astype(BF16)
    w_out_bf = w_out.astype(BF16)
    wq_bf = w_query.astype(BF16)
    sk_bf = sub_keys.astype(BF16)
    u_bf = expert_u.astype(BF16)
    vt_bf = expert_v.T.astype(BF16)

    rows = db + nb
    rpad = (-rows) % 8
    c_all = jnp.concatenate([c_sample, c_prompt, jnp.zeros((rpad, d), F32)], axis=0)
    mod = _ada(c_all, w_ada, b_ada)
    mod_p = mod.reshape(rows + rpad, 1, 6 * d)
    mod_s = mod.reshape(1, rows + rpad, 6 * d)

    m = nb * seq
    xp = x_prompt.reshape(m, d)
    tm = _pick(seq, 512)
    prow = lambda t: (lambda i: db + (i * t) // seq)
    proj_p, qkv_p = _inproj(xp, norm1_g, mod_p, prow(tm), 1, w_in_bf, tm)
    conv_zero = jnp.zeros((nb, CONV_W - 1, dc), F32)
    conv_p, conv_prompt = _conv_prompt(proj_p, conv_w, conv_zero, seq, tm)
    att_p = _attn_prompt(qkv_p, lam_p, slopes, subln_g, nb, seq, _pick(seq, 256))
    tm2 = _pick(seq, 256)
    x1_p, h2_p = _outproj(xp, conv_p, att_p, w_out_bf, norm2_g, mod_p, prow(tm2), 1, tm2)
    tt = _pick(seq, 512)
    pk_p = _peer_query(h2_p, wq_bf, sk_bf, tt)
    yt_p = _peer_dense(h2_p, *pk_p, u_bf, vt_bf, tt)
    y_prompt = _final(x1_p, yt_p, final_g, mod_p, prow(tt), 1, tt, tt).reshape(nb, seq, d)
    k_prompt = proj_p[:, 4 * dc:5 * dc].reshape(nb, seq, nh, 2 * DK)
    v_prompt = proj_p[:, 5 * dc:6 * dc].reshape(nb, seq, nh, DV)

    xs = x_sample.reshape(db, d)
    srow = lambda i: 0
    proj_s, qkv_s = _inproj(xs, norm1_g, mod_s, srow, db, w_in_bf, db)
    conv_s, z_s = _conv_sample(proj_s, conv_w, state_conv[:, 0], state_conv[:, 1])
    conv_sample = jnp.stack([state_conv[:, 1], z_s], axis=1)
    k_new = proj_s[:, 4 * dc:5 * dc]
    v_new = proj_s[:, 5 * dc:6 * dc]
    pages = 4 if page_table.shape[1] % 4 == 0 else 1
    a, a_new = _dec_scores(page_table, slopes, lam_p, qkv_s[:, :dc], k_new,
                           cache_k.reshape(n_pool, page, nh * 2 * DK), pages)
    att_s = _dec_pv(page_table, a, a_new, v_new, subln_g,
                    cache_v.reshape(n_pool, page, nh * DV), pages)
    x1_s, h2_s = _outproj(xs, conv_s, att_s, w_out_bf, norm2_g, mod_s, srow, db, db)
    tpad = (-db) % LANES
    h2_s_pad = jnp.concatenate([h2_s, jnp.zeros((tpad, d), BF16)], axis=0)
    tts = db + tpad
    pk_s = _peer_query(h2_s_pad, wq_bf, sk_bf, tts)
    yt_s = _peer_dense(h2_s_pad, *pk_s, u_bf, vt_bf, tts)
    y_sample = _final(x1_s, yt_s, final_g, mod_s, srow, db, db, tts).reshape(db, 1, d)
    k_sample = k_new.reshape(db, 1, nh, 2 * DK)
    v_sample = v_new.reshape(db, 1, nh, DV)

    return (y_prompt, y_sample, k_prompt, v_prompt, conv_prompt,
            k_sample, v_sample, conv_sample)
```

```python
import functools
import math

import jax
import jax.numpy as jnp
from jax import lax
from jax.experimental import pallas as pl
from jax.experimental.pallas import tpu as pltpu

F32 = jnp.float32
BF16 = jnp.bfloat16

N_ATT_HEADS = 8
DK = 64
DV = 128
CONV_W = 3
N_KEYS = 128
PK_HEADS = 8
PK_TOPK = 16
EPS = 1e-6
SUBLN_EPS = 1e-5
NEG_INF = -1e30
LAMBDA_INIT = 0.8 - 0.6 * math.exp(-0.3 * 0)
SQRT_HALF = 0.7071067811865476

LANES = 128
VMEM_LIMIT = 56 * 1024 * 1024


def _params(sem, vmem=VMEM_LIMIT):
    return pltpu.CompilerParams(dimension_semantics=sem, vmem_limit_bytes=vmem)


def _dot(a, b):
    return jnp.dot(a, b, preferred_element_type=F32)


def _dot_nt(a, b):
    return lax.dot_general(a, b, (((1,), (1,)), ((), ())), preferred_element_type=F32)


def _dot_tn(a, b):
    return lax.dot_general(a, b, (((0,), (0,)), ((), ())), preferred_element_type=F32)


def _rms(x, g, eps):
    return x * lax.rsqrt(jnp.mean(x * x, axis=-1, keepdims=True) + eps) * g


def _ada_kernel(c_ref, w_ref, b_ref, o_ref):
    c = c_ref[...]
    s = c * jax.nn.sigmoid(c)
    o_ref[...] = _dot(s.astype(BF16), w_ref[...].astype(BF16)) + b_ref[...]


def _ada(c_all, w_ada, b_ada, tn=1024):
    r, d = c_all.shape
    n = w_ada.shape[1]
    return pl.pallas_call(
        _ada_kernel,
        grid=(n // tn,),
        in_specs=[pl.BlockSpec((r, d), lambda j: (0, 0)),
                  pl.BlockSpec((d, tn), lambda j: (0, j)),
                  pl.BlockSpec((1, tn), lambda j: (0, j))],
        out_specs=pl.BlockSpec((r, tn), lambda j: (0, j)),
        out_shape=jax.ShapeDtypeStruct((r, n), F32),
        compiler_params=_params(("arbitrary",)),
    )(c_all, w_ada, b_ada.reshape(1, n))


def _inproj_kernel(x_ref, g_ref, sc_ref, sh_ref, w_ref, conv_ref, k_ref, v_ref, qkv_ref, h_scr):
    j = pl.program_id(1)

    @pl.when(j == 0)
    def _():
        y = _rms(x_ref[...], g_ref[...], EPS)
        h_scr[...] = (y * (1.0 + sc_ref[0]) + sh_ref[0]).astype(BF16)

    p = _dot(h_scr[...], w_ref[...])

    @pl.when(j < 3)
    def _():
        conv_ref[...] = p

    @pl.when(j == 3)
    def _():
        qkv_ref[...] = (p * (DK ** -0.5)).astype(BF16)

    @pl.when(j == 4)
    def _():
        k_ref[...] = p
        qkv_ref[...] = p.astype(BF16)

    @pl.when(j == 5)
    def _():
        v_ref[...] = p
        qkv_ref[...] = p.astype(BF16)


def _inproj(x, g, mod3, mod_row, r, w_bf, tm):
    m, d = x.shape
    n = w_bf.shape[1]
    tn = n // 6
    mod_spec = lambda c: pl.BlockSpec((1, r, d), lambda i, j: (mod_row(i), 0, c))
    return pl.pallas_call(
        _inproj_kernel,
        grid=(m // tm, 6),
        in_specs=[pl.BlockSpec((tm, d), lambda i, j: (i, 0)),
                  pl.BlockSpec((1, d), lambda i, j: (0, 0)),
                  mod_spec(1), mod_spec(0),
                  pl.BlockSpec((d, tn), lambda i, j: (0, j))],
        out_specs=[pl.BlockSpec((tm, tn), lambda i, j: (i, jnp.minimum(j, 2))),
                   pl.BlockSpec((tm, tn), lambda i, j: (i, 0)),
                   pl.BlockSpec((tm, tn), lambda i, j: (i, 0)),
                   pl.BlockSpec((tm, tn), lambda i, j: (i, jnp.maximum(j - 3, 0)))],
        out_shape=[jax.ShapeDtypeStruct((m, n // 2), F32),
                   jax.ShapeDtypeStruct((m, tn), F32),
                   jax.ShapeDtypeStruct((m, tn), F32),
                   jax.ShapeDtypeStruct((m, n // 2), BF16)],
        scratch_shapes=[pltpu.VMEM((tm, d), BF16)],
        compiler_params=_params(("arbitrary", "arbitrary")),
    )(x, g.reshape(1, d), mod3, mod3, w_bf)


def _conv_prompt_kernel(hc_ref, gb_ref, gc_ref, w_ref, st_ref, o_ref, new_ref, z_scr, *, tm, tiles_per_seq):
    i = pl.program_id(0)
    pad = 8

    @pl.when(i % tiles_per_seq == 0)
    def _():
        z_scr[pad - 2:pad, :] = st_ref[0]

    z = gc_ref[...] * hc_ref[...]
    z_scr[pad:pad + tm, :] = z
    w = w_ref[...]
    conv = (w[0:1] * z_scr[pad - 2:pad - 2 + tm, :]
            + w[1:2] * z_scr[pad - 1:pad - 1 + tm, :]
            + w[2:3] * z)
    o_ref[...] = (gb_ref[...] * conv).astype(BF16)
    last = z_scr[pad + tm - 2:pad + tm, :]
    new_ref[0] = last
    z_scr[pad - 2:pad, :] = last


def _conv_prompt(conv3, conv_w, state, seq, tm):
    m = conv3.shape[0]
    dc = conv_w.shape[1]
    nb = state.shape[0]
    tps = seq // tm
    col = lambda c: pl.BlockSpec((tm, dc), lambda i: (i, c))
    return pl.pallas_call(
        functools.partial(_conv_prompt_kernel, tm=tm, tiles_per_seq=tps),
        grid=(m // tm,),
        in_specs=[col(0), col(1), col(2),
                  pl.BlockSpec((CONV_W, dc), lambda i: (0, 0)),
                  pl.BlockSpec((1, CONV_W - 1, dc), lambda i: (i // tps, 0, 0))],
        out_specs=[pl.BlockSpec((tm, dc), lambda i: (i, 0)),
                   pl.BlockSpec((1, CONV_W - 1, dc), lambda i: (i // tps, 0, 0))],
        out_shape=[jax.ShapeDtypeStruct((m, dc), BF16),
                   jax.ShapeDtypeStruct((nb, CONV_W - 1, dc), F32)],
        scratch_shapes=[pltpu.VMEM((tm + 8, dc), F32)],
        compiler_params=_params(("arbitrary",)),
    )(conv3, conv3, conv3, conv_w, state)


def _conv_sample_kernel(hc_ref, gb_ref, gc_ref, w_ref, s0_ref, s1_ref, o_ref, z_ref):
    z = gc_ref[...] * hc_ref[...]
    w = w_ref[...]
    conv = w[0:1] * s0_ref[...] + w[1:2] * s1_ref[...] + w[2:3] * z
    o_ref[...] = (gb_ref[...] * conv).astype(BF16)
    z_ref[...] = z


def _conv_sample(conv3, conv_w, s0, s1):
    m = conv3.shape[0]
    dc = conv_w.shape[1]
    col = lambda c: pl.BlockSpec((m, dc), lambda i: (0, c))
    full = pl.BlockSpec((m, dc), lambda i: (0, 0))
    return pl.pallas_call(
        _conv_sample_kernel,
        grid=(1,),
        in_specs=[col(0), col(1), col(2),
                  pl.BlockSpec((CONV_W, dc), lambda i: (0, 0)), full, full],
        out_specs=[full, full],
        out_shape=[jax.ShapeDtypeStruct((m, dc), BF16),
                   jax.ShapeDtypeStruct((m, dc), F32)],
        compiler_params=_params(("arbitrary",)),
    )(conv3, conv3, conv3, conv_w, s0, s1)


def _lambda(lam_ref):
    lp = lam_ref[...]
    e1 = jnp.exp(jnp.sum(lp[0:1] * lp[1:2], axis=-1, keepdims=True))
    e2 = jnp.exp(jnp.sum(lp[2:3] * lp[3:4], axis=-1, keepdims=True))
    return e1 - e2 + LAMBDA_INIT


def _attn_prompt_kernel(slope_ref, lam_ref, q_ref, k_ref, v_ref, g_ref, o_ref,
                        qq_scr, m_scr, l_scr, acc_scr, *, tq, hps):
    hg = pl.program_id(1)
    i = pl.program_id(2)
    krow = lax.broadcasted_iota(jnp.int32, (tq, 2 * tq), 0)
    qcol = lax.broadcasted_iota(jnp.int32, (tq, 2 * tq), 1)
    qcol = jnp.where(qcol >= tq, qcol - tq, qcol)
    krow_f = krow.astype(F32)
    lane = lax.broadcasted_iota(jnp.int32, (tq, 2 * DK), 1)
    slopes = [slope_ref[hg * hps + hh] for hh in range(hps)]
    for hh in range(hps):
        q = q_ref[:, hh * 2 * DK:(hh + 1) * 2 * DK]
        zero = jnp.zeros_like(q)
        qq_scr[hh, 0:tq, :] = jnp.where(lane < DK, q, zero)
        qq_scr[hh, tq:2 * tq, :] = jnp.where(lane >= DK, q, zero)
    m_scr[...] = jnp.full(m_scr.shape, NEG_INF, F32)
    l_scr[...] = jnp.zeros(l_scr.shape, F32)
    acc_scr[...] = jnp.zeros(acc_scr.shape, F32)

    def block(j, masked):
        start = pl.multiple_of(j * tq, tq)
        scores = [_dot_nt(k_ref[pl.ds(start, tq), hh * 2 * DK:(hh + 1) * 2 * DK], qq_scr[hh])
                  for hh in range(hps)]
        for hh in range(hps):
            vb = v_ref[pl.ds(start, tq), hh * DV:(hh + 1) * DV]
            s = scores[hh] + slopes[hh] * krow_f
            if masked:
                s = jnp.where(krow > qcol, NEG_INF, s)
            off = slopes[hh] * ((j - i) * tq).astype(F32)
            m_old = m_scr[hh]
            m_new = jnp.maximum(m_old, jnp.max(s, axis=0, keepdims=True) + off)
            p = jnp.exp(s - (m_new - off))
            alpha = jnp.exp(m_old - m_new)
            l_scr[hh] = alpha * l_scr[hh] + jnp.sum(p, axis=0, keepdims=True)
            acc_scr[hh] = alpha * acc_scr[hh] + _dot_tn(vb, p.astype(BF16))
            m_scr[hh] = m_new

    def body(j, carry):
        block(j, False)
        return carry

    lax.fori_loop(0, i, body, 0)
    block(i, True)

    lam = _lambda(lam_ref)
    for hh in range(hps):
        o = acc_scr[hh] / l_scr[hh]
        o = o[:, :tq] - lam * o[:, tq:]
        o = o * lax.rsqrt(jnp.mean(o * o, axis=0, keepdims=True) + SUBLN_EPS)
        o_ref[:, hh * DV:(hh + 1) * DV] = ((o.T * g_ref[...]) * (1.0 - LAMBDA_INIT)).astype(BF16)


def _attn_prompt(qkv, lam_p, slopes, subln_g, nb, seq, tq, hps=4):
    m = qkv.shape[0]
    nh = N_ATT_HEADS
    ng = nh // hps
    nq = seq // tq
    return pl.pallas_call(
        functools.partial(_attn_prompt_kernel, tq=tq, hps=hps),
        grid=(nb, ng, nq),
        in_specs=[pl.BlockSpec(memory_space=pltpu.SMEM),
                  pl.BlockSpec((4, DK), lambda b, h, i: (0, 0)),
                  pl.BlockSpec((tq, hps * 2 * DK), lambda b, h, i: (b * nq + i, h)),
                  pl.BlockSpec((seq, hps * 2 * DK), lambda b, h, i: (b, ng + h)),
                  pl.BlockSpec((seq, hps * DV), lambda b, h, i: (b, 2 * ng + h)),
                  pl.BlockSpec((1, DV), lambda b, h, i: (0, 0))],
        out_specs=pl.BlockSpec((tq, hps * DV), lambda b, h, i: (b * nq + i, h)),
        out_shape=jax.ShapeDtypeStruct((m, nh * DV), BF16),
        scratch_shapes=[pltpu.VMEM((hps, 2 * tq, 2 * DK), BF16),
                        pltpu.VMEM((hps, 1, 2 * tq), F32),
                        pltpu.VMEM((hps, 1, 2 * tq), F32),
                        pltpu.VMEM((hps, DV, 2 * tq), F32)],
        compiler_params=_params(("arbitrary", "arbitrary", "arbitrary")),
    )(slopes, lam_p, qkv, qkv, qkv, subln_g.reshape(1, DV))


def _decode_kernel(pt_ref, slope_ref, lam_ref, q_ref, kn_ref, vn_ref, g_ref, *rest,
                   pages, page, n_pages):
    k_refs = rest[:pages]
    v_refs = rest[pages:2 * pages]
    o_ref, s_scr, a_scr, an_scr, acc_scr = rest[2 * pages:]
    j = pl.program_id(1)
    nh = N_ATT_HEADS
    nj = n_pages // pages
    width = page * nh
    past = n_pages * page
    diag = (lax.broadcasted_iota(jnp.int32, (2 * nh, width), 0) % nh
            == lax.broadcasted_iota(jnp.int32, (2 * nh, width), 1) % nh)
    sub = lax.broadcasted_iota(jnp.int32, (nh, LANES), 0)
    lane = lax.broadcasted_iota(jnp.int32, (nh, LANES), 1)
    q8 = q_ref[0].astype(F32)

    def to_lanes(col):
        return jnp.sum(jnp.where(lane % nh == sub, col, 0.0), axis=0, keepdims=True)

    def to_rows(row):
        return jnp.sum(jnp.where(lane == sub, row, 0.0), axis=-1, keepdims=True)

    def per_head(row, op):
        r = row[:, 0:LANES]
        for c in range(1, width // LANES):
            r = op(r, row[:, c * LANES:(c + 1) * LANES])
        shift = nh
        while shift < LANES:
            r = op(r, pltpu.roll(r, shift, axis=1))
            shift *= 2
        return r

    def tile(row):
        return jnp.concatenate([row] * (width // LANES), axis=1)

    @pl.when(j < nj)
    def _():
        qm = jnp.concatenate([jnp.where(lane < DK, q8, 0.0),
                              jnp.where(lane >= DK, q8, 0.0)], axis=0).astype(BF16)
        for u in range(pages):
            kb = k_refs[u][0].reshape(width, 2 * DK).astype(BF16)
            st = jnp.where(diag, _dot_nt(qm, kb), 0.0)
            pg = j * pages + u
            s_scr[0, pl.ds(pg, 1), :] = jnp.sum(st[0:nh], axis=0, keepdims=True)
            s_scr[1, pl.ds(pg, 1), :] = jnp.sum(st[nh:], axis=0, keepdims=True)

    @pl.when(j == nj - 1)
    def _():
        kn8 = kn_ref[0].astype(BF16).astype(F32)
        prod = q8 * kn8
        sn = [to_lanes(jnp.sum(jnp.where(lane < DK, prod, 0.0), axis=-1, keepdims=True)),
              to_lanes(jnp.sum(jnp.where(lane >= DK, prod, 0.0), axis=-1, keepdims=True))]
        lw = lax.broadcasted_iota(jnp.int32, (n_pages, width), 1)
        kpos = lax.broadcasted_iota(jnp.int32, (n_pages, width), 0) * page + lw // nh
        slope = jnp.zeros((n_pages, width), F32)
        for hh in range(nh):
            slope = jnp.where(lw % nh == hh, slope_ref[hh], slope)
        bias = -slope * (past - kpos).astype(F32)
        lam = _lambda(lam_ref)
        a = None
        a_new = None
        for mi in range(2):
            s = s_scr[mi] + bias
            mx = jnp.maximum(per_head(jnp.max(s, axis=0, keepdims=True), jnp.maximum), sn[mi])
            p = jnp.exp(s - tile(mx))
            pn = jnp.exp(sn[mi] - mx)
            den = per_head(jnp.sum(p, axis=0, keepdims=True), jnp.add) + pn
            inv = 1.0 / den
            p = p * tile(inv)
            pn = pn * inv
            a = p if mi == 0 else a - lam * p
            a_new = pn if mi == 0 else a_new - lam * pn
        a_scr[...] = a
        an_scr[...] = jnp.broadcast_to(to_rows(a_new), (nh, LANES))
        acc_scr[...] = jnp.zeros(acc_scr.shape, F32)

    @pl.when(j >= nj)
    def _():
        acc = acc_scr[...]
        for u in range(pages):
            pg = (j - nj) * pages + u
            sel = jnp.where(diag, a_scr[pl.ds(pg, 1), :], 0.0).astype(BF16)
            vb = v_refs[u][0].reshape(width, DV).astype(BF16)
            acc = acc + _dot(sel, vb)
        acc_scr[...] = acc

    @pl.when(j == 2 * nj - 1)
    def _():
        o = acc_scr[0:nh, :] + an_scr[...] * vn_ref[0]
        o = o * lax.rsqrt(jnp.mean(o * o, axis=-1, keepdims=True) + SUBLN_EPS)
        o_ref[0] = (o * g_ref[...]) * (1.0 - LAMBDA_INIT)


def _decode(page_table, slopes, lam_p, q_s, k_new, v_new, subln_g, cache_k, cache_v, pages):
    db, n_pages = page_table.shape
    page, nh = cache_k.shape[1], cache_k.shape[2]
    nj = n_pages // pages
    width = page * nh
    kspec = lambda u: pl.BlockSpec(
        (1, page, nh, 2 * DK),
        lambda b, j, pt: (pt[b, jnp.minimum(j, nj - 1) * pages + u], 0, 0, 0))
    vspec = lambda u: pl.BlockSpec(
        (1, page, nh, DV),
        lambda b, j, pt: (pt[b, jnp.maximum(j - nj, 0) * pages + u], 0, 0, 0))
    tok = pl.BlockSpec((1, nh, 2 * DK), lambda b, j, pt: (b, 0, 0))
    grid_spec = pltpu.PrefetchScalarGridSpec(
        num_scalar_prefetch=1,
        grid=(db, 2 * nj),
        in_specs=[pl.BlockSpec(memory_space=pltpu.SMEM),
                  pl.BlockSpec((4, DK), lambda b, j, pt: (0, 0)),
                  tok, tok, tok,
                  pl.BlockSpec((1, DV), lambda b, j, pt: (0, 0))]
                 + [kspec(u) for u in range(pages)] + [vspec(u) for u in range(pages)],
        out_specs=pl.BlockSpec((1, nh, DV), lambda b, j, pt: (b, 0, 0)),
        scratch_shapes=[pltpu.VMEM((2, n_pages, width), F32),
                        pltpu.VMEM((n_pages, width), F32),
                        pltpu.VMEM((nh, LANES), F32),
                        pltpu.VMEM((2 * nh, DV), F32)],
    )
    return pl.pallas_call(
        functools.partial(_decode_kernel, pages=pages, page=page, n_pages=n_pages),
        grid_spec=grid_spec,
        out_shape=jax.ShapeDtypeStruct((db, nh, DV), F32),
        compiler_params=_params(("arbitrary", "arbitrary")),
    )(page_table, slopes, lam_p, q_s.reshape(db, nh, 2 * DK), k_new.reshape(db, nh, 2 * DK),
      v_new.reshape(db, nh, DV), subln_g.reshape(1, DV),
      *([cache_k] * pages), *([cache_v] * pages))


def _outproj_kernel(x_ref, conv_ref, att_ref, w_ref, g1_ref, ng_ref, sc_ref, sh_ref,
                    x1_ref, h2_ref):
    dc = conv_ref.shape[1]
    mixed = _dot(conv_ref[...], w_ref[0:dc, :]) + _dot(att_ref[...], w_ref[dc:, :])
    x1 = x_ref[...] + g1_ref[0] * mixed
    x1_ref[...] = x1
    y = _rms(x1, ng_ref[...], EPS)
    h2_ref[...] = (y * (1.0 + sc_ref[0]) + sh_ref[0]).astype(BF16)


def _outproj(x, conv_o, att, w_bf, g2n, mod3, mod_row, r, tm):
    m, d = x.shape
    dc = conv_o.shape[1]
    mod_spec = lambda c: pl.BlockSpec((1, r, d), lambda i: (mod_row(i), 0, c))
    half = pl.BlockSpec((tm, dc), lambda i: (i, 0))
    row = pl.BlockSpec((tm, d), lambda i: (i, 0))
    return pl.pallas_call(
        _outproj_kernel,
        grid=(m // tm,),
        in_specs=[row, half, half,
                  pl.BlockSpec((d, d), lambda i: (0, 0)),
                  mod_spec(2),
                  pl.BlockSpec((1, d), lambda i: (0, 0)),
                  mod_spec(4), mod_spec(3)],
        out_specs=[row, row],
        out_shape=[jax.ShapeDtypeStruct((m, d), F32), jax.ShapeDtypeStruct((m, d), BF16)],
        compiler_params=_params(("arbitrary",)),
    )(x, conv_o, att, w_bf, mod3, g2n.reshape(1, d), mod3, mod3)


def _top_desc(s, k):
    out = []
    for _ in range(k):
        mx = jnp.max(s, axis=0, keepdims=True)
        out.append(mx)
        s = jnp.where(s == mx, NEG_INF, s)
    return out


_CAND_PAIRS = [(i, j) for i in range(PK_TOPK) for j in range(PK_TOPK)
               if (i + 1) * (j + 1) <= PK_TOPK]
_CAND_ROWS = -(-len(_CAND_PAIRS) // 8) * 8


def _peer_query_kernel(h2_ref, wq_ref, sk_ref, s1_ref, s2_ref, a1_ref, a2_ref, tau_ref,
                       cand_scr):
    q = _dot(h2_ref[...], wq_ref[...]).astype(BF16)
    dq = sk_ref.shape[-1]
    cand_scr[...] = jnp.full(cand_scr.shape, NEG_INF, F32)
    for h in range(PK_HEADS):
        s1 = _dot_nt(sk_ref[h, 0], q[:, (2 * h) * dq:(2 * h + 1) * dq])
        s2 = _dot_nt(sk_ref[h, 1], q[:, (2 * h + 1) * dq:(2 * h + 2) * dq])
        t1 = _top_desc(s1, PK_TOPK)
        t2 = _top_desc(s2, PK_TOPK)
        for r, (i, j) in enumerate(_CAND_PAIRS):
            cand_scr[r:r + 1, :] = t1[i] + t2[j]
        c = _top_desc(cand_scr[...], PK_TOPK)
        z = jnp.ones_like(c[0])
        for kk in range(1, PK_TOPK):
            z = z + jnp.exp(c[kk] - c[0])
        s1_ref[h] = s1
        s2_ref[h] = s2
        a1_ref[h] = jnp.exp(s1 - t1[0]) / z
        a2_ref[h] = jnp.exp(s2 - t2[0])
        tau_ref[h] = c[PK_TOPK - 1]


def _peer_query(h2, wq_bf, sk_bf, tt):
    n, d = h2.shape
    nq = wq_bf.shape[1]
    big = pl.BlockSpec((PK_HEADS, N_KEYS, tt), lambda i: (0, 0, i))
    shp = jax.ShapeDtypeStruct((PK_HEADS, N_KEYS, n), F32)
    return pl.pallas_call(
        _peer_query_kernel,
        grid=(n // tt,),
        in_specs=[pl.BlockSpec((tt, d), lambda i: (i, 0)),
                  pl.BlockSpec((d, nq), lambda i: (0, 0)),
                  pl.BlockSpec(sk_bf.shape, lambda i: (0, 0, 0, 0))],
        out_specs=[big, big, big, big,
                   pl.BlockSpec((PK_HEADS, 1, tt), lambda i: (0, 0, i))],
        out_shape=[shp, shp, shp, shp, jax.ShapeDtypeStruct((PK_HEADS, 1, n), F32)],
        scratch_shapes=[pltpu.VMEM((_CAND_ROWS, tt), F32)],
        compiler_params=_params(("arbitrary",)),
    )(h2, wq_bf, sk_bf)


def _peer_dense_kernel(h2_ref, s1_ref, s2_ref, a1_ref, a2_ref, tau_ref, u_ref, v_ref,
                       y_ref, act_scr, w_scr, *, rows):
    c = pl.program_id(1)
    tt = act_scr.shape[1]
    per = 2 * LANES // N_KEYS

    @pl.when(c == 0)
    def _():
        y_ref[...] = jnp.zeros(y_ref.shape, F32)

    act_scr[...] = _dot_nt(u_ref[...], h2_ref[...])
    for e in range(rows):
        es = slice(e * N_KEYS, (e + 1) * N_KEYS)
        for c0 in range(0, tt, LANES):
            cs = slice(c0, c0 + LANES)
            gate = jnp.zeros((N_KEYS, LANES), F32)
            for h in range(PK_HEADS):
                hit = s1_ref[h, e:e + 1, cs] + s2_ref[h, :, cs] >= tau_ref[h, :, cs]
                gate = gate + jnp.where(hit, a2_ref[h, :, cs], 0.0) * a1_ref[h, e:e + 1, cs]
            x = act_scr[es, cs]
            gelu = 0.5 * x * (1.0 + lax.erf(x * SQRT_HALF))
            w_scr[cs, es] = (gate * gelu).T.astype(BF16)
        if (e + 1) % per == 0:
            ks = slice((e + 1 - per) * N_KEYS, (e + 1) * N_KEYS)
            y_ref[...] += _dot(w_scr[:, ks], v_ref[ks, :])


def _peer_dense(h2, s1, s2, a1, a2, tau, u_bf, v_bf, tt, rows=8):
    n, d = h2.shape
    ne = u_bf.shape[0]
    eb = rows * N_KEYS
    big = pl.BlockSpec((PK_HEADS, N_KEYS, tt), lambda i, c: (0, 0, i))
    sub = pl.BlockSpec((PK_HEADS, rows, tt), lambda i, c: (0, c, i))
    experts = pl.BlockSpec((eb, d), lambda i, c: (c, 0))
    return pl.pallas_call(
        functools.partial(_peer_dense_kernel, rows=rows),
        grid=(n // tt, ne // eb),
        in_specs=[pl.BlockSpec((tt, d), lambda i, c: (i, 0)),
                  sub, big, sub, big,
                  pl.BlockSpec((PK_HEADS, 1, tt), lambda i, c: (0, 0, i)),
                  experts, experts],
        out_specs=pl.BlockSpec((tt, d), lambda i, c: (i, 0)),
        out_shape=jax.ShapeDtypeStruct((n, d), F32),
        scratch_shapes=[pltpu.VMEM((eb, tt), F32), pltpu.VMEM((tt, eb), BF16)],
        compiler_params=_params(("arbitrary", "arbitrary")),
    )(h2, s1, s2, a1, a2, tau, u_bf, v_bf)


def _final_kernel(x1_ref, y_ref, g2_ref, fg_ref, o_ref):
    o_ref[...] = _rms(x1_ref[...] + g2_ref[0] * y_ref[...], fg_ref[...], EPS)


def _final(x1, y, final_g, mod3, mod_row, r, tm):
    m, d = x1.shape
    row = pl.BlockSpec((tm, d), lambda i: (i, 0))
    return pl.pallas_call(
        _final_kernel,
        grid=(m // tm,),
        in_specs=[row, row,
                  pl.BlockSpec((1, r, d), lambda i: (mod_row(i), 0, 5)),
                  pl.BlockSpec((1, d), lambda i: (0, 0))],
        out_specs=row,
        out_shape=jax.ShapeDtypeStruct((m, d), F32),
        compiler_params=_params(("arbitrary",)),
    )(x1, y, mod3, final_g.reshape(1, d))


def _pick(n, pref):
    t = min(n, pref)
    assert n % t == 0, (n, t)
    return t


def kernel(x_prompt, x_sample, cache_k, cache_v, state_conv, page_table, c_prompt, c_sample,
           norm1_g, norm2_g, w_ada, b_ada, w_in, conv_w, lambda_q1, lambda_k1, lambda_q2,
           lambda_k2, subln_g, w_out, w_query, sub_keys, expert_u, expert_v, final_g):
    nb, seq, d = x_prompt.shape
    db = x_sample.shape[0]
    assert x_sample.shape[1] == 1
    nh = N_ATT_HEADS
    dc = conv_w.shape[1]
    assert w_in.shape[1] == 6 * dc and dc == nh * DV and d == 2 * dc
    assert cache_k.shape[2:] == (nh, 2 * DK) and cache_v.shape[2:] == (nh, DV)

    slopes = jnp.asarray(2.0 ** (-(8.0 / nh) * jnp.arange(1, nh + 1)), F32)
    lam_p = jnp.stack([lambda_q1, lambda_k1, lambda_q2, lambda_k2]).astype(F32)
    w_in_bf = w_in.astype(BF16)
    w_out_bf = w_out.astype(BF16)
    wq_bf = w_query.astype(BF16)
    sk_bf = sub_keys.astype(BF16)
    u_bf = expert_u.astype(BF16)
    v_bf = expert_v.astype(BF16)

    rows = db + nb
    rpad = (-rows) % 8
    c_all = jnp.concatenate([c_sample, c_prompt, jnp.zeros((rpad, d), F32)], axis=0)
    mod = _ada(c_all, w_ada, b_ada)
    mod_p = mod.reshape(rows + rpad, 1, 6 * d)
    mod_s = mod.reshape(1, rows + rpad, 6 * d)

    m = nb * seq
    xp = x_prompt.reshape(m, d)
    tm = _pick(seq, 512)
    prow = lambda t: (lambda i, *_: db + (i * t) // seq)
    conv3_p, k_p, v_p, qkv_p = _inproj(xp, norm1_g, mod_p, prow(tm), 1, w_in_bf, tm)
    conv_zero = jnp.zeros((nb, CONV_W - 1, dc), F32)
    conv_p, conv_prompt = _conv_prompt(conv3_p, conv_w, conv_zero, seq, tm)
    att_p = _attn_prompt(qkv_p, lam_p, slopes, subln_g, nb, seq, _pick(seq, 256))
    tm2 = _pick(seq, 256)
    x1_p, h2_p = _outproj(xp, conv_p, att_p, w_out_bf, norm2_g, mod_p, prow(tm2), 1, tm2)
    tt = _pick(seq, 512)
    pk_p = _peer_query(h2_p, wq_bf, sk_bf, tt)
    yy_p = _peer_dense(h2_p, *pk_p, u_bf, v_bf, tt)
    y_prompt = _final(x1_p, yy_p, final_g, mod_p, prow(tt), 1, tt).reshape(nb, seq, d)
    k_prompt = k_p.reshape(nb, seq, nh, 2 * DK)
    v_prompt = v_p.reshape(nb, seq, nh, DV)

    xs = x_sample.reshape(db, d)
    srow = lambda i, *_: 0
    conv3_s, k_new, v_new, qkv_s = _inproj(xs, norm1_g, mod_s, srow, db, w_in_bf, db)
    conv_s, z_s = _conv_sample(conv3_s, conv_w, state_conv[:, 0], state_conv[:, 1])
    conv_sample = jnp.stack([state_conv[:, 1], z_s], axis=1)
    n_pages = page_table.shape[1]
    pages = 8 if n_pages % 8 == 0 else 1
    att_s = _decode(page_table, slopes, lam_p, qkv_s[:, :dc], k_new, v_new, subln_g,
                    cache_k, cache_v, pages)
    att_s = att_s.reshape(db, nh * DV).astype(BF16)
    x1_s, h2_s = _outproj(xs, conv_s, att_s, w_out_bf, norm2_g, mod_s, srow, db, db)
    tpad = (-db) % LANES
    h2_s_pad = jnp.concatenate([h2_s, jnp.zeros((tpad, d), BF16)], axis=0)
    tts = db + tpad
    pk_s = _peer_query(h2_s_pad, wq_bf, sk_bf, tts)
    yy_s = _peer_dense(h2_s_pad, *pk_s, u_bf, v_bf, tts)
    y_sample = _final(x1_s, yy_s, final_g, mod_s, srow, db, db).reshape(db, 1, d)
    k_sample = k_new.reshape(db, 1, nh, 2 * DK)
    v_sample = v_new.reshape(db, 1, nh, DV)

    return (y_prompt, y_sample, k_prompt, v_prompt, conv_prompt,
            k_sample, v_sample, conv_sample)
```

```python
import functools
import math

import jax
import jax.numpy as jnp
from jax import lax
from jax.experimental import pallas as pl
from jax.experimental.pallas import tpu as pltpu

F32 = jnp.float32
BF16 = jnp.bfloat16

N_ATT_HEADS = 8
DK = 64
DV = 128
CONV_W = 3
N_KEYS = 128
PK_HEADS = 8
PK_TOPK = 16
EPS = 1e-6
SUBLN_EPS = 1e-5
NEG_INF = -1e30
LAMBDA_INIT = 0.8 - 0.6 * math.exp(-0.3 * 0)
SQRT_HALF = 0.7071067811865476

LANES = 128
VMEM_LIMIT = 56 * 1024 * 1024


def _params(sem, vmem=VMEM_LIMIT):
    return pltpu.CompilerParams(dimension_semantics=sem, vmem_limit_bytes=vmem)


def _dot(a, b):
    return jnp.dot(a, b, preferred_element_type=F32)


def _dot_nt(a, b):
    return lax.dot_general(a, b, (((1,), (1,)), ((), ())), preferred_element_type=F32)


def _dot_tn(a, b):
    return lax.dot_general(a, b, (((0,), (0,)), ((), ())), preferred_element_type=F32)


def _rms(x, g, eps):
    return x * lax.rsqrt(jnp.mean(x * x, axis=-1, keepdims=True) + eps) * g


def _ada_kernel(c_ref, w_ref, b_ref, o_ref):
    c = c_ref[...]
    s = c * jax.nn.sigmoid(c)
    o_ref[...] = _dot(s.astype(BF16), w_ref[...].astype(BF16)) + b_ref[...]


def _ada(c_all, w_ada, b_ada, tn=1024):
    r, d = c_all.shape
    n = w_ada.shape[1]
    return pl.pallas_call(
        _ada_kernel,
        grid=(n // tn,),
        in_specs=[pl.BlockSpec((r, d), lambda j: (0, 0)),
                  pl.BlockSpec((d, tn), lambda j: (0, j)),
                  pl.BlockSpec((1, tn), lambda j: (0, j))],
        out_specs=pl.BlockSpec((r, tn), lambda j: (0, j)),
        out_shape=jax.ShapeDtypeStruct((r, n), F32),
        compiler_params=_params(("arbitrary",)),
    )(c_all, w_ada, b_ada.reshape(1, n))


def _inproj_kernel(x_ref, g_ref, sc_ref, sh_ref, w_ref, cw_ref, p0_ref, p1_ref,
                   conv_ref, new_ref, k_ref, v_ref, qkv_ref, h_scr, hc_scr, gb_scr, z_scr,
                   *, tiles_per_seq):
    i = pl.program_id(0)
    j = pl.program_id(1)
    tm = x_ref.shape[0]

    @pl.when(j == 0)
    def _():
        y = _rms(x_ref[...], g_ref[...], EPS)
        h_scr[...] = (y * (1.0 + sc_ref[0]) + sh_ref[0]).astype(BF16)

    p = _dot(h_scr[...], w_ref[...])

    @pl.when(j == 0)
    def _():
        hc_scr[...] = p

    @pl.when(j == 1)
    def _():
        gb_scr[...] = p

    @pl.when(j == 2)
    def _():
        z = p * hc_scr[...]
        w = cw_ref[...]
        if tiles_per_seq:
            pad = 8

            @pl.when(i % tiles_per_seq == 0)
            def _():
                z_scr[pad - 2:pad, :] = p0_ref[0]

            z_scr[pad:pad + tm, :] = z
            conv = (w[0:1] * z_scr[pad - 2:pad - 2 + tm, :]
                    + w[1:2] * z_scr[pad - 1:pad - 1 + tm, :] + w[2:3] * z)
            last = z_scr[pad + tm - 2:pad + tm, :]
            new_ref[0] = last
            z_scr[pad - 2:pad, :] = last
        else:
            conv = w[0:1] * p0_ref[...] + w[1:2] * p1_ref[...] + w[2:3] * z
            new_ref[...] = z
        conv_ref[...] = (gb_scr[...] * conv).astype(BF16)

    @pl.when(j == 3)
    def _():
        qkv_ref[...] = (p * (DK ** -0.5)).astype(BF16)

    @pl.when(j == 4)
    def _():
        k_ref[...] = p
        qkv_ref[...] = p.astype(BF16)

    @pl.when(j == 5)
    def _():
        v_ref[...] = p
        qkv_ref[...] = p.astype(BF16)


def _inproj(x, g, mod3, mod_row, r, w_bf, conv_w, prev, tm, seq):
    m, d = x.shape
    n = w_bf.shape[1]
    tn = n // 6
    tps = seq // tm
    mod_spec = lambda c: pl.BlockSpec((1, r, d), lambda i, j: (mod_row(i), 0, c))
    tile = pl.BlockSpec((tm, tn), lambda i, j: (i, 0))
    if tps:
        state = pl.BlockSpec((1, CONV_W - 1, tn), lambda i, j: (i // tps, 0, 0))
        prev_specs, prev, new_spec = [state, state], (prev[0], prev[0]), state
        new_shape = jax.ShapeDtypeStruct(prev[0].shape, F32)
    else:
        prev_specs, new_spec = [tile, tile], tile
        new_shape = jax.ShapeDtypeStruct((m, tn), F32)
    return pl.pallas_call(
        functools.partial(_inproj_kernel, tiles_per_seq=tps),
        grid=(m // tm, 6),
        in_specs=[pl.BlockSpec((tm, d), lambda i, j: (i, 0)),
                  pl.BlockSpec((1, d), lambda i, j: (0, 0)),
                  mod_spec(1), mod_spec(0),
                  pl.BlockSpec((d, tn), lambda i, j: (0, j)),
                  pl.BlockSpec((CONV_W, tn), lambda i, j: (0, 0))] + prev_specs,
        out_specs=[tile, new_spec, tile, tile,
                   pl.BlockSpec((tm, tn), lambda i, j: (i, jnp.maximum(j - 3, 0)))],
        out_shape=[jax.ShapeDtypeStruct((m, tn), BF16), new_shape,
                   jax.ShapeDtypeStruct((m, tn), F32),
                   jax.ShapeDtypeStruct((m, tn), F32),
                   jax.ShapeDtypeStruct((m, n // 2), BF16)],
        scratch_shapes=[pltpu.VMEM((tm, d), BF16), pltpu.VMEM((tm, tn), F32),
                        pltpu.VMEM((tm, tn), F32), pltpu.VMEM((tm + 8, tn), F32)],
        compiler_params=_params(("arbitrary", "arbitrary")),
    )(x, g.reshape(1, d), mod3, mod3, w_bf, conv_w, *prev)


def _lambda(lam_ref):
    lp = lam_ref[...]
    e1 = jnp.exp(jnp.sum(lp[0:1] * lp[1:2], axis=-1, keepdims=True))
    e2 = jnp.exp(jnp.sum(lp[2:3] * lp[3:4], axis=-1, keepdims=True))
    return e1 - e2 + LAMBDA_INIT


def _attn_prompt_kernel(slope_ref, lam_ref, q_ref, k_ref, v_ref, g_ref, o_ref,
                        qq_scr, m_scr, l_scr, acc_scr, *, tq, hps):
    hg = pl.program_id(1)
    i = pl.program_id(2)
    krow = lax.broadcasted_iota(jnp.int32, (tq, 2 * tq), 0)
    qcol = lax.broadcasted_iota(jnp.int32, (tq, 2 * tq), 1)
    qcol = jnp.where(qcol >= tq, qcol - tq, qcol)
    krow_f = krow.astype(F32)
    lane = lax.broadcasted_iota(jnp.int32, (tq, 2 * DK), 1)
    slopes = [slope_ref[hg * hps + hh] for hh in range(hps)]
    for hh in range(hps):
        q = q_ref[:, hh * 2 * DK:(hh + 1) * 2 * DK]
        zero = jnp.zeros_like(q)
        qq_scr[hh, 0:tq, :] = jnp.where(lane < DK, q, zero)
        qq_scr[hh, tq:2 * tq, :] = jnp.where(lane >= DK, q, zero)
    m_scr[...] = jnp.full(m_scr.shape, NEG_INF, F32)
    l_scr[...] = jnp.zeros(l_scr.shape, F32)
    acc_scr[...] = jnp.zeros(acc_scr.shape, F32)

    def block(j, masked):
        start = pl.multiple_of(j * tq, tq)
        scores = [_dot_nt(k_ref[pl.ds(start, tq), hh * 2 * DK:(hh + 1) * 2 * DK], qq_scr[hh])
                  for hh in range(hps)]
        for hh in range(hps):
            vb = v_ref[pl.ds(start, tq), hh * DV:(hh + 1) * DV]
            s = scores[hh] + slopes[hh] * krow_f
            if masked:
                s = jnp.where(krow > qcol, NEG_INF, s)
            off = slopes[hh] * ((j - i) * tq).astype(F32)
            m_old = m_scr[hh]
            m_new = jnp.maximum(m_old, jnp.max(s, axis=0, keepdims=True) + off)
            p = jnp.exp(s - (m_new - off))
            alpha = jnp.exp(m_old - m_new)
            l_scr[hh] = alpha * l_scr[hh] + jnp.sum(p, axis=0, keepdims=True)
            acc_scr[hh] = alpha * acc_scr[hh] + _dot_tn(vb, p.astype(BF16))
            m_scr[hh] = m_new

    def body(j, carry):
        block(j, False)
        return carry

    lax.fori_loop(0, i, body, 0)
    block(i, True)

    lam = _lambda(lam_ref)
    for hh in range(hps):
        o = acc_scr[hh] / l_scr[hh]
        o = o[:, :tq] - lam * o[:, tq:]
        o = o * lax.rsqrt(jnp.mean(o * o, axis=0, keepdims=True) + SUBLN_EPS)
        o_ref[:, hh * DV:(hh + 1) * DV] = ((o.T * g_ref[...]) * (1.0 - LAMBDA_INIT)).astype(BF16)


def _attn_prompt(qkv, lam_p, slopes, subln_g, nb, seq, tq, hps=8):
    m = qkv.shape[0]
    nh = N_ATT_HEADS
    ng = nh // hps
    nq = seq // tq
    return pl.pallas_call(
        functools.partial(_attn_prompt_kernel, tq=tq, hps=hps),
        grid=(nb, ng, nq),
        in_specs=[pl.BlockSpec(memory_space=pltpu.SMEM),
                  pl.BlockSpec((4, DK), lambda b, h, i: (0, 0)),
                  pl.BlockSpec((tq, hps * 2 * DK), lambda b, h, i: (b * nq + i, h)),
                  pl.BlockSpec((seq, hps * 2 * DK), lambda b, h, i: (b, ng + h)),
                  pl.BlockSpec((seq, hps * DV), lambda b, h, i: (b, 2 * ng + h)),
                  pl.BlockSpec((1, DV), lambda b, h, i: (0, 0))],
        out_specs=pl.BlockSpec((tq, hps * DV), lambda b, h, i: (b * nq + i, h)),
        out_shape=jax.ShapeDtypeStruct((m, nh * DV), BF16),
        scratch_shapes=[pltpu.VMEM((hps, 2 * tq, 2 * DK), BF16),
                        pltpu.VMEM((hps, 1, 2 * tq), F32),
                        pltpu.VMEM((hps, 1, 2 * tq), F32),
                        pltpu.VMEM((hps, DV, 2 * tq), F32)],
        compiler_params=_params(("arbitrary", "arbitrary", "arbitrary")),
    )(slopes, lam_p, qkv, qkv, qkv, subln_g.reshape(1, DV))


def _decode_kernel(pt_ref, slope_ref, lam_ref, q_ref, kn_ref, vn_ref, g_ref, *rest,
                   pages, page, n_pages):
    k_refs = rest[:pages]
    v_refs = rest[pages:2 * pages]
    o_ref, s_scr, a_scr, an_scr, acc_scr = rest[2 * pages:]
    j = pl.program_id(1)
    nh = N_ATT_HEADS
    nj = n_pages // pages
    width = page * nh
    past = n_pages * page
    diag = (lax.broadcasted_iota(jnp.int32, (2 * nh, width), 0) % nh
            == lax.broadcasted_iota(jnp.int32, (2 * nh, width), 1) % nh)
    sub = lax.broadcasted_iota(jnp.int32, (nh, LANES), 0)
    lane = lax.broadcasted_iota(jnp.int32, (nh, LANES), 1)
    q8 = q_ref[0].astype(F32)

    def to_lanes(col):
        return jnp.sum(jnp.where(lane % nh == sub, col, 0.0), axis=0, keepdims=True)

    def to_rows(row):
        return jnp.sum(jnp.where(lane == sub, row, 0.0), axis=-1, keepdims=True)

    def per_head(row, op):
        r = row[:, 0:LANES]
        for c in range(1, width // LANES):
            r = op(r, row[:, c * LANES:(c + 1) * LANES])
        shift = nh
        while shift < LANES:
            r = op(r, pltpu.roll(r, shift, axis=1))
            shift *= 2
        return r

    def tile(row):
        return jnp.concatenate([row] * (width // LANES), axis=1)

    @pl.when(j < nj)
    def _():
        qm = jnp.concatenate([jnp.where(lane < DK, q8, 0.0),
                              jnp.where(lane >= DK, q8, 0.0)], axis=0).astype(BF16)
        for u in range(pages):
            kb = k_refs[u][0].reshape(width, 2 * DK).astype(BF16)
            st = jnp.where(diag, _dot_nt(qm, kb), 0.0)
            pg = j * pages + u
            s_scr[0, pl.ds(pg, 1), :] = jnp.sum(st[0:nh], axis=0, keepdims=True)
            s_scr[1, pl.ds(pg, 1), :] = jnp.sum(st[nh:], axis=0, keepdims=True)

    @pl.when(j == nj - 1)
    def _():
        kn8 = kn_ref[0].astype(BF16).astype(F32)
        prod = q8 * kn8
        sn = [to_lanes(jnp.sum(jnp.where(lane < DK, prod, 0.0), axis=-1, keepdims=True)),
              to_lanes(jnp.sum(jnp.where(lane >= DK, prod, 0.0), axis=-1, keepdims=True))]
        lw = lax.broadcasted_iota(jnp.int32, (n_pages, width), 1)
        kpos = lax.broadcasted_iota(jnp.int32, (n_pages, width), 0) * page + lw // nh
        slope = jnp.zeros((n_pages, width), F32)
        for hh in range(nh):
            slope = jnp.where(lw % nh == hh, slope_ref[hh], slope)
        bias = -slope * (past - kpos).astype(F32)
        lam = _lambda(lam_ref)
        a = None
        a_new = None
        for mi in range(2):
            s = s_scr[mi] + bias
            mx = jnp.maximum(per_head(jnp.max(s, axis=0, keepdims=True), jnp.maximum), sn[mi])
            p = jnp.exp(s - tile(mx))
            pn = jnp.exp(sn[mi] - mx)
            den = per_head(jnp.sum(p, axis=0, keepdims=True), jnp.add) + pn
            inv = 1.0 / den
            p = p * tile(inv)
            pn = pn * inv
            a = p if mi == 0 else a - lam * p
            a_new = pn if mi == 0 else a_new - lam * pn
        a_scr[...] = a
        an_scr[...] = jnp.broadcast_to(to_rows(a_new), (nh, LANES))
        acc_scr[...] = jnp.zeros(acc_scr.shape, F32)

    @pl.when(j >= nj)
    def _():
        acc = acc_scr[...]
        for u in range(pages):
            pg = (j - nj) * pages + u
            sel = jnp.where(diag, a_scr[pl.ds(pg, 1), :], 0.0).astype(BF16)
            vb = v_refs[u][0].reshape(width, DV).astype(BF16)
            acc = acc + _dot(sel, vb)
        acc_scr[...] = acc

    @pl.when(j == 2 * nj - 1)
    def _():
        o = acc_scr[0:nh, :] + an_scr[...] * vn_ref[0]
        o = o * lax.rsqrt(jnp.mean(o * o, axis=-1, keepdims=True) + SUBLN_EPS)
        o_ref[0] = (o * g_ref[...]) * (1.0 - LAMBDA_INIT)


def _decode(page_table, slopes, lam_p, q_s, k_new, v_new, subln_g, cache_k, cache_v, pages):
    db, n_pages = page_table.shape
    page, nh = cache_k.shape[1], cache_k.shape[2]
    nj = n_pages // pages
    width = page * nh
    kspec = lambda u: pl.BlockSpec(
        (1, page, nh, 2 * DK),
        lambda b, j, pt: (pt[b, jnp.minimum(j, nj - 1) * pages + u], 0, 0, 0))
    vspec = lambda u: pl.BlockSpec(
        (1, page, nh, DV),
        lambda b, j, pt: (pt[b, jnp.maximum(j - nj, 0) * pages + u], 0, 0, 0))
    tok = pl.BlockSpec((1, nh, 2 * DK), lambda b, j, pt: (b, 0, 0))
    grid_spec = pltpu.PrefetchScalarGridSpec(
        num_scalar_prefetch=1,
        grid=(db, 2 * nj),
        in_specs=[pl.BlockSpec(memory_space=pltpu.SMEM),
                  pl.BlockSpec((4, DK), lambda b, j, pt: (0, 0)),
                  tok, tok, tok,
                  pl.BlockSpec((1, DV), lambda b, j, pt: (0, 0))]
                 + [kspec(u) for u in range(pages)] + [vspec(u) for u in range(pages)],
        out_specs=pl.BlockSpec((1, nh, DV), lambda b, j, pt: (b, 0, 0)),
        scratch_shapes=[pltpu.VMEM((2, n_pages, width), F32),
                        pltpu.VMEM((n_pages, width), F32),
                        pltpu.VMEM((nh, LANES), F32),
                        pltpu.VMEM((2 * nh, DV), F32)],
    )
    return pl.pallas_call(
        functools.partial(_decode_kernel, pages=pages, page=page, n_pages=n_pages),
        grid_spec=grid_spec,
        out_shape=jax.ShapeDtypeStruct((db, nh, DV), F32),
        compiler_params=_params(("arbitrary", "arbitrary")),
    )(page_table, slopes, lam_p, q_s.reshape(db, nh, 2 * DK), k_new.reshape(db, nh, 2 * DK),
      v_new.reshape(db, nh, DV), subln_g.reshape(1, DV),
      *([cache_k] * pages), *([cache_v] * pages))


def _outproj_kernel(x_ref, conv_ref, att_ref, w_ref, g1_ref, ng_ref, sc_ref, sh_ref,
                    x1_ref, h2_ref):
    dc = conv_ref.shape[1]
    mixed = _dot(conv_ref[...], w_ref[0:dc, :]) + _dot(att_ref[...], w_ref[dc:, :])
    x1 = x_ref[...] + g1_ref[0] * mixed
    x1_ref[...] = x1
    y = _rms(x1, ng_ref[...], EPS)
    h2_ref[...] = (y * (1.0 + sc_ref[0]) + sh_ref[0]).astype(BF16)


def _outproj(x, conv_o, att, w_bf, g2n, mod3, mod_row, r, tm):
    m, d = x.shape
    dc = conv_o.shape[1]
    mod_spec = lambda c: pl.BlockSpec((1, r, d), lambda i: (mod_row(i), 0, c))
    half = pl.BlockSpec((tm, dc), lambda i: (i, 0))
    row = pl.BlockSpec((tm, d), lambda i: (i, 0))
    return pl.pallas_call(
        _outproj_kernel,
        grid=(m // tm,),
        in_specs=[row, half, half,
                  pl.BlockSpec((d, d), lambda i: (0, 0)),
                  mod_spec(2),
                  pl.BlockSpec((1, d), lambda i: (0, 0)),
                  mod_spec(4), mod_spec(3)],
        out_specs=[row, row],
        out_shape=[jax.ShapeDtypeStruct((m, d), F32), jax.ShapeDtypeStruct((m, d), BF16)],
        compiler_params=_params(("arbitrary",)),
    )(x, conv_o, att, w_bf, mod3, g2n.reshape(1, d), mod3, mod3)


def _top_desc(s, k):
    out = []
    for _ in range(k):
        mx = jnp.max(s, axis=0, keepdims=True)
        out.append(mx)
        s = jnp.where(s == mx, NEG_INF, s)
    return out


_CAND_PAIRS = [(i, j) for i in range(PK_TOPK) for j in range(PK_TOPK)
               if (i + 1) * (j + 1) <= PK_TOPK]
_CAND_ROWS = -(-len(_CAND_PAIRS) // 8) * 8


def _peer_query_kernel(h2_ref, wq_ref, sk_ref, s1_ref, s2_ref, a1_ref, a2_ref, tau_ref,
                       cand_scr):
    q = _dot(h2_ref[...], wq_ref[...]).astype(BF16)
    dq = sk_ref.shape[-1]
    cand_scr[...] = jnp.full(cand_scr.shape, NEG_INF, F32)
    for h in range(PK_HEADS):
        s1 = _dot_nt(sk_ref[h, 0], q[:, (2 * h) * dq:(2 * h + 1) * dq])
        s2 = _dot_nt(sk_ref[h, 1], q[:, (2 * h + 1) * dq:(2 * h + 2) * dq])
        t1 = _top_desc(s1, PK_TOPK)
        t2 = _top_desc(s2, PK_TOPK)
        for r, (i, j) in enumerate(_CAND_PAIRS):
            cand_scr[r:r + 1, :] = t1[i] + t2[j]
        c = _top_desc(cand_scr[...], PK_TOPK)
        z = jnp.ones_like(c[0])
        for kk in range(1, PK_TOPK):
            z = z + jnp.exp(c[kk] - c[0])
        s1_ref[h] = s1
        s2_ref[h] = s2
        a1_ref[h] = jnp.exp(s1 - t1[0]) / z
        a2_ref[h] = jnp.exp(s2 - t2[0])
        tau_ref[h] = c[PK_TOPK - 1]


def _peer_query(h2, wq_bf, sk_bf, tt):
    n, d = h2.shape
    nq = wq_bf.shape[1]
    big = pl.BlockSpec((PK_HEADS, N_KEYS, tt), lambda i: (0, 0, i))
    shp = jax.ShapeDtypeStruct((PK_HEADS, N_KEYS, n), F32)
    return pl.pallas_call(
        _peer_query_kernel,
        grid=(n // tt,),
        in_specs=[pl.BlockSpec((tt, d), lambda i: (i, 0)),
                  pl.BlockSpec((d, nq), lambda i: (0, 0)),
                  pl.BlockSpec(sk_bf.shape, lambda i: (0, 0, 0, 0))],
        out_specs=[big, big, big, big,
                   pl.BlockSpec((PK_HEADS, 1, tt), lambda i: (0, 0, i))],
        out_shape=[shp, shp, shp, shp, jax.ShapeDtypeStruct((PK_HEADS, 1, n), F32)],
        scratch_shapes=[pltpu.VMEM((_CAND_ROWS, tt), F32)],
        compiler_params=_params(("arbitrary",)),
    )(h2, wq_bf, sk_bf)


def _peer_dense_kernel(h2_ref, s1_ref, s2_ref, a1_ref, a2_ref, tau_ref, u_ref, v_ref,
                       y_ref, act_scr, w_scr, *, rows):
    c = pl.program_id(1)
    tt = act_scr.shape[0]
    per = 2 * LANES // N_KEYS
    n_pieces = rows // per

    @pl.when(c == 0)
    def _():
        y_ref[...] = jnp.zeros(y_ref.shape, F32)

    def experts(p):
        return slice(p * per * N_KEYS, (p + 1) * per * N_KEYS)

    def activations(p):
        act_scr[:, experts(p)] = _dot_nt(h2_ref[...], u_ref[experts(p), :])

    activations(0)
    for p in range(n_pieces):
        if p + 1 < n_pieces:
            activations(p + 1)
        for e in range(p * per, (p + 1) * per):
            es = slice(e * N_KEYS, (e + 1) * N_KEYS)
            for c0 in range(0, tt, LANES):
                cs = slice(c0, c0 + LANES)
                gate = jnp.zeros((N_KEYS, LANES), F32)
                for h in range(PK_HEADS):
                    hit = s1_ref[h, e:e + 1, cs] + s2_ref[h, :, cs] >= tau_ref[h, :, cs]
                    gate = gate + jnp.where(hit, a2_ref[h, :, cs], 0.0) * a1_ref[h, e:e + 1, cs]
                x = act_scr[cs, es]
                gelu = 0.5 * x * (1.0 + lax.erf(x * SQRT_HALF))
                w_scr[cs, es] = (gate.T * gelu).astype(BF16)
        y_ref[...] += _dot(w_scr[:, experts(p)], v_ref[experts(p), :])


def _peer_dense(h2, s1, s2, a1, a2, tau, u_bf, v_bf, tt, rows=8):
    n, d = h2.shape
    ne = u_bf.shape[0]
    eb = rows * N_KEYS
    big = pl.BlockSpec((PK_HEADS, N_KEYS, tt), lambda i, c: (0, 0, i))
    sub = pl.BlockSpec((PK_HEADS, rows, tt), lambda i, c: (0, c, i))
    experts = pl.BlockSpec((eb, d), lambda i, c: (c, 0))
    return pl.pallas_call(
        functools.partial(_peer_dense_kernel, rows=rows),
        grid=(n // tt, ne // eb),
        in_specs=[pl.BlockSpec((tt, d), lambda i, c: (i, 0)),
                  sub, big, sub, big,
                  pl.BlockSpec((PK_HEADS, 1, tt), lambda i, c: (0, 0, i)),
                  experts, experts],
        out_specs=pl.BlockSpec((tt, d), lambda i, c: (i, 0)),
        out_shape=jax.ShapeDtypeStruct((n, d), F32),
        scratch_shapes=[pltpu.VMEM((tt, eb), F32), pltpu.VMEM((tt, eb), BF16)],
        compiler_params=_params(("arbitrary", "arbitrary")),
    )(h2, s1, s2, a1, a2, tau, u_bf, v_bf)


def _final_kernel(x1_ref, y_ref, g2_ref, fg_ref, o_ref):
    o_ref[...] = _rms(x1_ref[...] + g2_ref[0] * y_ref[...], fg_ref[...], EPS)


def _final(x1, y, final_g, mod3, mod_row, r, tm):
    m, d = x1.shape
    row = pl.BlockSpec((tm, d), lambda i: (i, 0))
    return pl.pallas_call(
        _final_kernel,
        grid=(m // tm,),
        in_specs=[row, row,
                  pl.BlockSpec((1, r, d), lambda i: (mod_row(i), 0, 5)),
                  pl.BlockSpec((1, d), lambda i: (0, 0))],
        out_specs=row,
        out_shape=jax.ShapeDtypeStruct((m, d), F32),
        compiler_params=_params(("arbitrary",)),
    )(x1, y, mod3, final_g.reshape(1, d))


def _pick(n, pref):
    t = min(n, pref)
    assert n % t == 0, (n, t)
    return t


def kernel(x_prompt, x_sample, cache_k, cache_v, state_conv, page_table, c_prompt, c_sample,
           norm1_g, norm2_g, w_ada, b_ada, w_in, conv_w, lambda_q1, lambda_k1, lambda_q2,
           lambda_k2, subln_g, w_out, w_query, sub_keys, expert_u, expert_v, final_g):
    nb, seq, d = x_prompt.shape
    db = x_sample.shape[0]
    assert x_sample.shape[1] == 1
    nh = N_ATT_HEADS
    dc = conv_w.shape[1]
    assert w_in.shape[1] == 6 * dc and dc == nh * DV and d == 2 * dc
    assert cache_k.shape[2:] == (nh, 2 * DK) and cache_v.shape[2:] == (nh, DV)

    slopes = jnp.asarray(2.0 ** (-(8.0 / nh) * jnp.arange(1, nh + 1)), F32)
    lam_p = jnp.stack([lambda_q1, lambda_k1, lambda_q2, lambda_k2]).astype(F32)
    w_in_bf = w_in.astype(BF16)
    w_out_bf = w_out.astype(BF16)
    wq_bf = w_query.astype(BF16)
    sk_bf = sub_keys.astype(BF16)
    u_bf = expert_u.astype(BF16)
    v_bf = expert_v.astype(BF16)

    rows = db + nb
    rpad = (-rows) % 8
    c_all = jnp.concatenate([c_sample, c_prompt, jnp.zeros((rpad, d), F32)], axis=0)
    mod = _ada(c_all, w_ada, b_ada)
    mod_p = mod.reshape(rows + rpad, 1, 6 * d)
    mod_s = mod.reshape(1, rows + rpad, 6 * d)

    m = nb * seq
    xp = x_prompt.reshape(m, d)
    tm = _pick(seq, 512)
    prow = lambda t: (lambda i, *_: db + (i * t) // seq)
    conv_zero = jnp.zeros((nb, CONV_W - 1, dc), F32)
    conv_p, conv_prompt, k_p, v_p, qkv_p = _inproj(
        xp, norm1_g, mod_p, prow(tm), 1, w_in_bf, conv_w, (conv_zero,), tm, seq)
    att_p = _attn_prompt(qkv_p, lam_p, slopes, subln_g, nb, seq, _pick(seq, 256))
    tm2 = _pick(seq, 256)
    x1_p, h2_p = _outproj(xp, conv_p, att_p, w_out_bf, norm2_g, mod_p, prow(tm2), 1, tm2)
    tt = _pick(seq, 512)
    pk_p = _peer_query(h2_p, wq_bf, sk_bf, tt)
    yy_p = _peer_dense(h2_p, *pk_p, u_bf, v_bf, tt)
    y_prompt = _final(x1_p, yy_p, final_g, mod_p, prow(tt), 1, tt).reshape(nb, seq, d)
    k_prompt = k_p.reshape(nb, seq, nh, 2 * DK)
    v_prompt = v_p.reshape(nb, seq, nh, DV)

    xs = x_sample.reshape(db, d)
    srow = lambda i, *_: 0
    conv_s, z_s, k_new, v_new, qkv_s = _inproj(
        xs, norm1_g, mod_s, srow, db, w_in_bf, conv_w,
        (state_conv[:, 0], state_conv[:, 1]), db, 0)
    conv_sample = jnp.stack([state_conv[:, 1], z_s], axis=1)
    n_pages = page_table.shape[1]
    pages = 16 if n_pages % 16 == 0 else 1
    att_s = _decode(page_table, slopes, lam_p, qkv_s[:, :dc], k_new, v_new, subln_g,
                    cache_k, cache_v, pages)
    att_s = att_s.reshape(db, nh * DV).astype(BF16)
    x1_s, h2_s = _outproj(xs, conv_s, att_s, w_out_bf, norm2_g, mod_s, srow, db, db)
    tpad = (-db) % LANES
    h2_s_pad = jnp.concatenate([h2_s, jnp.zeros((tpad, d), BF16)], axis=0)
    tts = db + tpad
    pk_s = _peer_query(h2_s_pad, wq_bf, sk_bf, tts)
    yy_s = _peer_dense(h2_s_pad, *pk_s, u_bf, v_bf, tts)
    y_sample = _final(x1_s, yy_s, final_g, mod_s, srow, db, db).reshape(db, 1, d)
    k_sample = k_new.reshape(db, 1, nh, 2 * DK)
    v_sample = v_new.reshape(db, 1, nh, DV)

    return (y_prompt, y_sample, k_prompt, v_prompt, conv_prompt,
            k_sample, v_sample, conv_sample)
```
